```python
import jax, jax.numpy as jnp
from jax import lax
import numpy as np

D_MODEL = 1024
BATCH = 2
SEQ = 8192
DEPTH = 4
DEC_BATCH = 128
DEC_SEQ = 1
PAST_LEN = 2048
PAGE_SIZE = 128

N_MIXERS = 2
HEAD_DIM = 64
MIX_WIDTH = D_MODEL
X_HEADS = 4
X_WIDTH = X_HEADS * HEAD_DIM
TOK_WIDTH = MIX_WIDTH - X_WIDTH
FOX_HEADS = TOK_WIDTH // HEAD_DIM
CONV_WIDTH = 3
N_MEM = 256
Q_BLOCK = 128
N_CONV_LAYERS = (DEPTH + 1) // 2
N_FOX_LAYERS = DEPTH // 2
ALPHA = (2 * DEPTH) ** 0.25
BETA = (8 * DEPTH) ** -0.25
LN_EPS = 1e-5
CONV_SPLIT = [TOK_WIDTH, TOK_WIDTH, TOK_WIDTH, X_WIDTH, MIX_WIDTH]
FOX_SPLIT = [TOK_WIDTH, TOK_WIDTH, TOK_WIDTH, FOX_HEADS, X_WIDTH, MIX_WIDTH]
CONV_IN_COLS = sum(CONV_SPLIT)
FOX_IN_COLS = sum(FOX_SPLIT)

kernel_name = 'fox_shortconv_hybrid_step'

F32 = jnp.float32


def _split(z, sizes):
    idx = [int(i) for i in np.cumsum(sizes)[:-1]]
    return jnp.split(z, idx, axis=-1)


def _heads(t, n):
    return t.reshape(t.shape[:-1] + (n, HEAD_DIM))


def _layer_norm(x, g, b):
    xf = x.astype(F32)
    mu = jnp.mean(xf, -1, keepdims=True)
    var = jnp.mean(jnp.square(xf - mu), -1, keepdims=True)
    y = (xf - mu) * lax.rsqrt(var + LN_EPS) * g.astype(F32) + b.astype(F32)
    return y.astype(x.dtype)


def _mem_kv(mem, w):
    k, v = _split(mem @ w, [X_WIDTH, X_WIDTH])
    return _heads(k, X_HEADS), _heads(v, X_HEADS)


def _cross(qm, mk, mv):
    s = jnp.einsum('bqhd,bkhd->bhqk', qm.astype(F32), mk.astype(F32)) * HEAD_DIM ** -0.5
    p = jax.nn.softmax(s, axis=-1)
    return jnp.einsum('bhqk,bkhd->bqhd', p, mv.astype(F32)).astype(qm.dtype)


def _short_conv(b_gate, c_gate, h, conv_w, prev):
    u = c_gate * h
    ext = jnp.concatenate([prev.astype(u.dtype), u], axis=1)
    n = u.shape[1]
    y = sum(conv_w[j] * ext[:, j:j + n] for j in range(CONV_WIDTH))
    return b_gate * y, ext[:, ext.shape[1] - (CONV_WIDTH - 1):]


def _fox_core(q_blk, dq_blk, q_pos, k_f, v_f, d_t):
    s = jnp.einsum('bqhd,bkhd->bhqk', q_blk.astype(F32), k_f) * HEAD_DIM ** -0.5
    s = s + jnp.swapaxes(dq_blk, 1, 2)[..., None] - d_t[:, :, None, :]
    k_pos = jnp.arange(k_f.shape[1], dtype=jnp.int32)
    s = jnp.where(k_pos[None, :] <= q_pos[:, None], s, -jnp.inf)
    p = jax.nn.softmax(s, axis=-1)
    return jnp.einsum('bhqk,bkhd->bqhd', p, v_f)


def _fox_prompt(q, k, v, logf):
    bsz, seq = q.shape[0], q.shape[1]
    nb = seq // Q_BLOCK
    d = jnp.cumsum(logf, axis=1)
    d_t = jnp.swapaxes(d, 1, 2)
    k_f = k.astype(F32)
    v_f = v.astype(F32)
    q_b = jnp.swapaxes(q.reshape(bsz, nb, Q_BLOCK, FOX_HEADS, HEAD_DIM), 0, 1)
    d_b = jnp.swapaxes(d.reshape(bsz, nb, Q_BLOCK, FOX_HEADS), 0, 1)
    starts = jnp.arange(nb, dtype=jnp.int32) * Q_BLOCK

    def block(args):
        q_i, d_i, s0 = args
        return _fox_core(q_i, d_i, s0 + jnp.arange(Q_BLOCK, dtype=jnp.int32), k_f, v_f, d_t)

    o = lax.map(block, (q_b, d_b, starts))
    return jnp.swapaxes(o, 0, 1).reshape(bsz, seq, FOX_HEADS, HEAD_DIM).astype(q.dtype)


def _fox_sample(q, k, v, logf, k_pool, v_pool, lf_pool, page_table):
    bsz, n_new = q.shape[0], q.shape[1]
    past = page_table.shape[1] * PAGE_SIZE
    k_past = k_pool[page_table].reshape(bsz, past, FOX_HEADS, HEAD_DIM)
    v_past = v_pool[page_table].reshape(bsz, past, FOX_HEADS, HEAD_DIM)
    lf_past = lf_pool[page_table].reshape(bsz, past, FOX_HEADS)
    k_f = jnp.concatenate([k_past.astype(F32), k.astype(F32)], axis=1)
    v_f = jnp.concatenate([v_past.astype(F32), v.astype(F32)], axis=1)
    d = jnp.cumsum(jnp.concatenate([lf_past.astype(F32), logf], axis=1), axis=1)
    q_pos = past + jnp.arange(n_new, dtype=jnp.int32)
    o = _fox_core(q, d[:, past:], q_pos, k_f, v_f, jnp.swapaxes(d, 1, 2))
    return o.astype(q.dtype)


def _finish(x, tok, crs, gate, w_out, g, b):
    mix = jnp.concatenate([tok.reshape(tok.shape[0], tok.shape[1], TOK_WIDTH),
                           crs.reshape(crs.shape[0], crs.shape[1], X_WIDTH)], axis=-1)
    mix = mix * jax.nn.silu(gate)
    return _layer_norm(ALPHA * x + mix @ w_out, g, b)


def setup_inputs(seed: int = 0) -> dict:
    key = jax.random.key(seed)
    ks = jax.random.split(key, 20)
    n_pages = PAST_LEN // PAGE_SIZE
    n_used = DEC_BATCH * n_pages
    n_phys = n_used + (n_used + 3) // 4
    nrm = jax.random.normal
    x_prompt = nrm(ks[0], (BATCH, SEQ, D_MODEL), F32)
    x_sample = nrm(ks[1], (DEC_BATCH, DEC_SEQ, D_MODEL), F32)
    mem_prompt = nrm(ks[2], (BATCH, N_MEM, D_MODEL), F32)
    cache_fox_k = nrm(ks[3], (N_FOX_LAYERS, n_phys, PAGE_SIZE, FOX_HEADS, HEAD_DIM), F32)
    cache_fox_v = nrm(ks[4], (N_FOX_LAYERS, n_phys, PAGE_SIZE, FOX_HEADS, HEAD_DIM), F32)
    cache_fox_logf = jax.nn.log_sigmoid(nrm(ks[5], (N_FOX_LAYERS, n_phys, PAGE_SIZE, FOX_HEADS), F32) + 2.0)
    state_conv = nrm(ks[6], (N_CONV_LAYERS, DEC_BATCH, CONV_WIDTH - 1, TOK_WIDTH), F32)
    cache_mem_k = nrm(ks[7], (DEPTH, DEC_BATCH, N_MEM, X_HEADS, HEAD_DIM), F32)
    cache_mem_v = nrm(ks[8], (DEPTH, DEC_BATCH, N_MEM, X_HEADS, HEAD_DIM), F32)
    page_table = jax.random.permutation(ks[9], n_phys)[:n_used].reshape(DEC_BATCH, n_pages).astype(jnp.int32)
    w_in_conv = nrm(ks[10], (N_CONV_LAYERS, D_MODEL, CONV_IN_COLS), F32) * D_MODEL ** -0.5
    conv_w = nrm(ks[11], (N_CONV_LAYERS, CONV_WIDTH, TOK_WIDTH), F32) * CONV_WIDTH ** -0.5
    w_in_fox = nrm(ks[12], (N_FOX_LAYERS, D_MODEL, FOX_IN_COLS), F32) * D_MODEL ** -0.5
    b_forget = 0.1 * nrm(ks[13], (N_FOX_LAYERS, FOX_HEADS), F32)
    w_mem_kv = nrm(ks[14], (DEPTH, D_MODEL, 2 * X_WIDTH), F32) * D_MODEL ** -0.5
    w_out = nrm(ks[15], (DEPTH, MIX_WIDTH, D_MODEL), F32) * (MIX_WIDTH ** -0.5 * BETA)
    ln_g = 1.0 + 0.02 * nrm(ks[16], (DEPTH, D_MODEL), F32)
    ln_b = 0.02 * nrm(ks[17], (DEPTH, D_MODEL), F32)
    return {'x_prompt': x_prompt, 'x_sample': x_sample, 'mem_prompt': mem_prompt,
            'cache_fox_k': cache_fox_k, 'cache_fox_v': cache_fox_v, 'cache_fox_logf': cache_fox_logf,
            'state_conv': state_conv, 'cache_mem_k': cache_mem_k, 'cache_mem_v': cache_mem_v,
            'page_table': page_table, 'w_in_conv': w_in_conv, 'conv_w': conv_w, 'w_in_fox': w_in_fox,
            'b_forget': b_forget, 'w_mem_kv': w_mem_kv, 'w_out': w_out, 'ln_g': ln_g, 'ln_b': ln_b}


def reference(x_prompt, x_sample, mem_prompt, cache_fox_k, cache_fox_v, cache_fox_logf, state_conv,
              cache_mem_k, cache_mem_v, page_table, w_in_conv, conv_w, w_in_fox, b_forget, w_mem_kv,
              w_out, ln_g, ln_b):
    xp, xs = x_prompt, x_sample
    fk_p, fv_p, fl_p, fk_s, fv_s, fl_s = [], [], [], [], [], []
    cv_p, cv_s, mk_list, mv_list = [], [], [], []
    for i in range(DEPTH):
        j = i // N_MIXERS
        mk_p, mv_p = _mem_kv(mem_prompt, w_mem_kv[i])
        mk_list.append(mk_p)
        mv_list.append(mv_p)
        if i % N_MIXERS == 0:
            b_p, c_p, h_p, qm_p, g_p = _split(xp @ w_in_conv[j], CONV_SPLIT)
            b_s, c_s, h_s, qm_s, g_s = _split(xs @ w_in_conv[j], CONV_SPLIT)
            zeros_prev = jnp.zeros((xp.shape[0], CONV_WIDTH - 1, TOK_WIDTH), xp.dtype)
            tok_p, st_p = _short_conv(b_p, c_p, h_p, conv_w[j], zeros_prev)
            tok_s, st_s = _short_conv(b_s, c_s, h_s, conv_w[j], state_conv[j])
            cv_p.append(st_p)
            cv_s.append(st_s)
        else:
            qa_p, ka_p, va_p, f_p, qm_p, g_p = _split(xp @ w_in_fox[j], FOX_SPLIT)
            qa_s, ka_s, va_s, f_s, qm_s, g_s = _split(xs @ w_in_fox[j], FOX_SPLIT)
            lf_p = jax.nn.log_sigmoid(f_p.astype(F32) + b_forget[j].astype(F32))
            lf_s = jax.nn.log_sigmoid(f_s.astype(F32) + b_forget[j].astype(F32))
            ka_p, va_p = _heads(ka_p, FOX_HEADS), _heads(va_p, FOX_HEADS)
            ka_s, va_s = _heads(ka_s, FOX_HEADS), _heads(va_s, FOX_HEADS)
            tok_p = _fox_prompt(_heads(qa_p, FOX_HEADS), ka_p, va_p, lf_p)
            tok_s = _fox_sample(_heads(qa_s, FOX_HEADS), ka_s, va_s, lf_s,
                                cache_fox_k[j], cache_fox_v[j], cache_fox_logf[j], page_table)
            fk_p.append(ka_p)
            fv_p.append(va_p)
            fl_p.append(lf_p.astype(xp.dtype))
            fk_s.append(ka_s)
            fv_s.append(va_s)
            fl_s.append(lf_s.astype(xs.dtype))
        crs_p = _cross(_heads(qm_p, X_HEADS), mk_p, mv_p)
        crs_s = _cross(_heads(qm_s, X_HEADS), cache_mem_k[i], cache_mem_v[i])
        xp = _finish(xp, tok_p, crs_p, g_p, w_out[i], ln_g[i], ln_b[i])
        xs = _finish(xs, tok_s, crs_s, g_s, w_out[i], ln_g[i], ln_b[i])
    return (xp, xs,
            jnp.stack(fk_p), jnp.stack(fv_p), jnp.stack(fl_p),
            jnp.stack(fk_s), jnp.stack(fv_s), jnp.stack(fl_s),
            jnp.stack(cv_p), jnp.stack(cv_s),
            jnp.stack(mk_list), jnp.stack(mv_list))
```

```python
import functools

import jax
import jax.numpy as jnp
from jax import lax
from jax.experimental import pallas as pl
from jax.experimental.pallas import tpu as pltpu

F32 = jnp.float32
BF16 = jnp.bfloat16

HEAD_DIM = 64
X_HEADS = 4
X_WIDTH = X_HEADS * HEAD_DIM
CONV_WIDTH = 3
LN_EPS = 1e-5
QK_SCALE = HEAD_DIM ** -0.5
LANES = 128
VMEM_LIMIT = 56 * 1024 * 1024

_NT = (((1,), (1,)), ((), ()))


def _dot(a, b):
    return jnp.dot(a, b, preferred_element_type=F32)


def _dot_nt(a, b):
    return lax.dot_general(a, b, _NT, preferred_element_type=F32)


def _silu(g):
    return g * jax.nn.sigmoid(g)


def _log_sigmoid(x):
    return jnp.minimum(x, 0.0) - jnp.log1p(jnp.exp(-jnp.abs(x)))


def _layer_norm(y, g, b, axis):
    mu = jnp.mean(y, axis=axis, keepdims=True)
    yc = y - mu
    var = jnp.mean(yc * yc, axis=axis, keepdims=True)
    return yc * lax.rsqrt(var + LN_EPS) * g + b


def _split3(x):
    hi = x.astype(BF16)
    r1 = x - hi.astype(F32)
    mid = r1.astype(BF16)
    lo = (r1 - mid.astype(F32)).astype(BF16)
    return hi, mid, lo


def _const_spec(shape):
    nd = len(shape)
    return pl.BlockSpec(shape, lambda *_: (0,) * nd, pipeline_mode=pl.Buffered(1))


def _mem_kv_kernel(mem_ref, wT_ref, kT_ref, vT_ref):
    memb = mem_ref[0].astype(BF16)
    kvT = _dot_nt(wT_ref[0], memb)
    kT_ref[0, 0] = kvT[:X_WIDTH]
    vT_ref[0, 0] = kvT[X_WIDTH:]


def _mem_kv(mem_prompt, w_mem_kvT):
    depth = w_mem_kvT.shape[0]
    bsz, n_mem, d_model = mem_prompt.shape
    out = jax.ShapeDtypeStruct((depth, bsz, X_WIDTH, n_mem), F32)
    return pl.pallas_call(
        _mem_kv_kernel,
        grid=(depth, bsz),
        in_specs=[pl.BlockSpec((1, n_mem, d_model), lambda i, b: (b, 0, 0)),
                  pl.BlockSpec((1, 2 * X_WIDTH, d_model), lambda i, b: (i, 0, 0))],
        out_specs=[pl.BlockSpec((1, 1, X_WIDTH, n_mem), lambda i, b: (i, b, 0, 0))] * 2,
        out_shape=[out, out],
        compiler_params=pltpu.CompilerParams(dimension_semantics=("arbitrary", "arbitrary")),
        name="mem_kv",
    )(mem_prompt, w_mem_kvT)


def _cross_prompt(qm, kT, vT):
    lane = lax.broadcasted_iota(jnp.int32, qm.shape, 1)
    vrow = lax.broadcasted_iota(jnp.int32, vT.shape, 0)
    out = None
    for h in range(X_HEADS):
        lo, hi = h * HEAD_DIM, (h + 1) * HEAD_DIM
        qh = jnp.where((lane >= lo) & (lane < hi), qm, 0.0).astype(BF16)
        s = _dot(qh, kT)
        m = jnp.max(s, axis=-1, keepdims=True)
        e = jnp.exp(s - m)
        p = e * (1.0 / jnp.sum(e, axis=-1, keepdims=True))
        vh = jnp.where((vrow >= lo) & (vrow < hi), vT, jnp.zeros_like(vT))
        o = _dot_nt(p.astype(BF16), vh)
        out = o if out is None else out + o
    return out


def _finish_prompt(x, tok, crs, sg, wout_ref, g, b, alpha):
    tw = tok.shape[1]
    mt = (tok * sg[:, :tw]).astype(BF16)
    mc = (crs * sg[:, tw:]).astype(BF16)
    y = alpha * x + _dot(mt, wout_ref[0:tw, :]) + _dot(mc, wout_ref[tw:, :])
    return _layer_norm(y, g, b, axis=-1)


def _conv_layer_kernel(x_ref, win_ref, cw_ref, mkT_ref, mvT_ref, wout_ref, g_ref, b_ref,
                       xo_ref, st_ref, ubuf, *, alpha, tw):
    t = pl.program_id(1)
    tm = x_ref.shape[1]

    @pl.when(t == 0)
    def _():
        ubuf[0:8, :] = jnp.zeros((8, tw), F32)

    x = x_ref[0]
    xb = x.astype(BF16)
    bg = _dot(xb, win_ref[:, 0:tw])
    u = _dot(xb, win_ref[:, tw:2 * tw]) * _dot(xb, win_ref[:, 2 * tw:3 * tw])
    ubuf[8:8 + tm, :] = u
    cw = cw_ref[...]
    y = cw[0:1] * ubuf[6:6 + tm, :] + cw[1:2] * ubuf[7:7 + tm, :] + cw[2:3] * u
    tok = bg * y
    last = ubuf[tm + 6:tm + 8, :]
    ubuf[6:8, :] = last
    st_ref[0] = last

    qm = _dot(xb, win_ref[:, 3 * tw:3 * tw + X_WIDTH]) * QK_SCALE
    gate = _dot(xb, win_ref[:, 3 * tw + X_WIDTH:])
    crs = _cross_prompt(qm, mkT_ref[0].astype(BF16), mvT_ref[0].astype(BF16))
    xo_ref[0] = _finish_prompt(x, tok, crs, _silu(gate), wout_ref, g_ref[...], b_ref[...], alpha)


def _conv_layer_prompt(x, win, cw, mkT, mvT, wout, g, b, *, alpha, tm):
    bsz, seq, d_model = x.shape
    tw = cw.shape[1]
    n_mem = mkT.shape[2]
    kern = functools.partial(_conv_layer_kernel, alpha=alpha, tw=tw)
    return pl.pallas_call(
        kern,
        grid=(bsz, seq // tm),
        in_specs=[pl.BlockSpec((1, tm, d_model), lambda bb, t: (bb, t, 0)),
                  _const_spec(win.shape),
                  _const_spec(cw.shape),
                  pl.BlockSpec((1, X_WIDTH, n_mem), lambda bb, t: (bb, 0, 0)),
                  pl.BlockSpec((1, X_WIDTH, n_mem), lambda bb, t: (bb, 0, 0)),
                  _const_spec(wout.shape),
                  _const_spec(g.shape),
                  _const_spec(b.shape)],
        out_specs=[pl.BlockSpec((1, tm, d_model), lambda bb, t: (bb, t, 0)),
                   pl.BlockSpec((1, CONV_WIDTH - 1, tw), lambda bb, t: (bb, 0, 0))],
        out_shape=[jax.ShapeDtypeStruct(x.shape, F32),
                   jax.ShapeDtypeStruct((bsz, CONV_WIDTH - 1, tw), F32)],
        scratch_shapes=[pltpu.VMEM((tm + 8, tw), F32)],
        compiler_params=pltpu.CompilerParams(dimension_semantics=("arbitrary", "arbitrary"),
                                             vmem_limit_bytes=VMEM_LIMIT),
        name="conv_layer_prompt",
    )(x, win, cw, mkT, mvT, wout, g, b)


def _fox_proj_kernel(x_ref, wn_ref, wkvT_ref, wfT_ref, bf_ref,
                     q_ref, kT_ref, vT_ref, kTb_ref, vTb_ref, lfT_ref, d_ref, dTb_ref, qm_ref, sg_ref,
                     carry_ref, tri_ref, *, tw, nh):
    t = pl.program_id(1)
    tm = x_ref.shape[1]
    hp = wfT_ref.shape[0]

    @pl.when(t == 0)
    def _():
        carry_ref[...] = jnp.zeros(carry_ref.shape, F32)
        r = lax.broadcasted_iota(jnp.int32, (tm, tm), 0)
        c = lax.broadcasted_iota(jnp.int32, (tm, tm), 1)
        tri_ref[...] = jnp.where(r <= c, 1.0, 0.0).astype(BF16)

    xb = x_ref[0].astype(BF16)
    q_ref[0] = (_dot(xb, wn_ref[:, 0:tw]) * QK_SCALE).astype(BF16)
    qm_ref[0] = (_dot(xb, wn_ref[:, tw:tw + X_WIDTH]) * QK_SCALE).astype(BF16)
    sg_ref[0] = _silu(_dot(xb, wn_ref[:, tw + X_WIDTH:])).astype(BF16)

    kT = _dot_nt(wkvT_ref[0:tw, :], xb)
    vT = _dot_nt(wkvT_ref[tw:2 * tw, :], xb)
    kT_ref[0] = kT
    vT_ref[0] = vT
    kTb = kT.astype(BF16)
    vTb = vT.astype(BF16)
    for h in range(nh):
        kTb_ref[0, 0, h] = kTb[h * HEAD_DIM:(h + 1) * HEAD_DIM, :]
        vTb_ref[0, 0, h] = vTb[h * HEAD_DIM:(h + 1) * HEAD_DIM, :]

    fT = _dot_nt(wfT_ref[...], xb) + bf_ref[...]
    hrow = lax.broadcasted_iota(jnp.int32, (hp, tm), 0)
    lf = jnp.where(hrow < nh, _log_sigmoid(fT), 0.0)
    lfT_ref[0] = lf

    cs3 = _dot(jnp.concatenate(_split3(lf), axis=0), tri_ref[...])
    cs = cs3[0:hp] + cs3[hp:2 * hp] + cs3[2 * hp:3 * hp] + carry_ref[:, 0:1]
    carry_ref[...] = jnp.broadcast_to(cs[:, tm - 1:tm], carry_ref.shape)
    for h in range(hp):
        dTb_ref[0, 0, h] = cs[h:h + 1, :]
    d_pad = jnp.concatenate([cs, jnp.zeros((LANES - hp, tm), F32)], axis=0)
    d_ref[0] = d_pad.T


def _fox_proj_prompt(x, wn, wkvT, wfT, bf, *, tm, nh):
    bsz, seq, d_model = x.shape
    tw = nh * HEAD_DIM
    hp = wfT.shape[0]
    nblk = seq // tm
    kern = functools.partial(_fox_proj_kernel, tw=tw, nh=nh)
    row = lambda bb, t: (bb, t, 0)
    colT = lambda bb, t: (bb, 0, t)
    out_shape = [
        jax.ShapeDtypeStruct((bsz, seq, tw), BF16),
        jax.ShapeDtypeStruct((bsz, tw, seq), F32),
        jax.ShapeDtypeStruct((bsz, tw, seq), F32),
        jax.ShapeDtypeStruct((bsz, nblk, nh, HEAD_DIM, tm), BF16),
        jax.ShapeDtypeStruct((bsz, nblk, nh, HEAD_DIM, tm), BF16),
        jax.ShapeDtypeStruct((bsz, hp, seq), F32),
        jax.ShapeDtypeStruct((bsz, seq, LANES), F32),
        jax.ShapeDtypeStruct((bsz, nblk, hp, 1, tm), F32),
        jax.ShapeDtypeStruct((bsz, seq, X_WIDTH), BF16),
        jax.ShapeDtypeStruct((bsz, seq, d_model), BF16),
    ]
    out_specs = [
        pl.BlockSpec((1, tm, tw), row),
        pl.BlockSpec((1, tw, tm), colT),
        pl.BlockSpec((1, tw, tm), colT),
        pl.BlockSpec((1, 1, nh, HEAD_DIM, tm), lambda bb, t: (bb, t, 0, 0, 0)),
        pl.BlockSpec((1, 1, nh, HEAD_DIM, tm), lambda bb, t: (bb, t, 0, 0, 0)),
        pl.BlockSpec((1, hp, tm), colT),
        pl.BlockSpec((1, tm, LANES), row),
        pl.BlockSpec((1, 1, hp, 1, tm), lambda bb, t: (bb, t, 0, 0, 0)),
        pl.BlockSpec((1, tm, X_WIDTH), row),
        pl.BlockSpec((1, tm, d_model), row),
    ]
    return pl.pallas_call(
        kern,
        grid=(bsz, nblk),
        in_specs=[pl.BlockSpec((1, tm, d_model), row),
                  _const_spec(wn.shape), _const_spec(wkvT.shape), _const_spec(wfT.shape),
                  _const_spec(bf.shape)],
        out_specs=out_specs,
        out_shape=out_shape,
        scratch_shapes=[pltpu.VMEM((hp, LANES), F32), pltpu.VMEM((tm, tm), BF16)],
        compiler_params=pltpu.CompilerParams(dimension_semantics=("arbitrary", "arbitrary"),
                                             vmem_limit_bytes=VMEM_LIMIT),
        name="fox_proj_prompt",
    )(x, wn, wkvT, wfT, bf)


def _fox_attn_kernel(q_ref, kTb_ref, vTb_ref, d_ref, dTb_ref, x_ref, qm_ref, sg_ref, mkT_ref, mvT_ref,
                     wout_ref, g_ref, b_ref, xo_ref,
                     qh_scr, dq_scr, m_scr, l_scr, acc_scr, tok_scr, *, alpha, nh):
    i = pl.program_id(1)
    tq = q_ref.shape[1]
    tk = kTb_ref.shape[4]
    reps = tk // LANES

    for h in range(nh):
        qh_scr[h] = q_ref[0, :, h * HEAD_DIM:(h + 1) * HEAD_DIM]
        dq_scr[h] = jnp.broadcast_to(d_ref[0, :, h:h + 1], (tq, LANES))

    row = lax.broadcasted_iota(jnp.int32, (tq, tk), 0)
    col = lax.broadcasted_iota(jnp.int32, (tq, tk), 1)
    causal = col <= row

    def head_body(h, carry):
        m_scr[...] = jnp.full(m_scr.shape, -jnp.inf, F32)
        l_scr[...] = jnp.zeros(l_scr.shape, F32)
        acc_scr[...] = jnp.zeros(acc_scr.shape, F32)
        qh = qh_scr[h]
        dq = dq_scr[h]
        dqt = jnp.concatenate([dq] * reps, axis=1)

        def step(j, masked):
            s = _dot(qh, kTb_ref[0, j, h])
            s = s + dqt - dTb_ref[0, j, h]
            if masked:
                s = jnp.where(causal, s, -jnp.inf)
            m_old = m_scr[...]
            m_new = jnp.maximum(m_old, jnp.max(s, axis=1, keepdims=True))
            a = jnp.exp(m_old - m_new)
            p = jnp.exp(s - jnp.concatenate([m_new] * reps, axis=1))
            l_scr[...] = a * l_scr[...] + jnp.sum(p, axis=1, keepdims=True)
            m_scr[...] = m_new
            pv = _dot_nt(p.astype(BF16), vTb_ref[0, j, h])
            acc_scr[...] = a[:, 0:HEAD_DIM] * acc_scr[...] + pv

        def loop_body(j, c):
            step(j, False)
            return c

        lax.fori_loop(0, i, loop_body, 0)
        step(i, True)
        tok_scr[h] = acc_scr[...] / l_scr[:, 0:HEAD_DIM]
        return carry

    lax.fori_loop(0, nh, head_body, 0)

    tok = jnp.concatenate([tok_scr[h] for h in range(nh)], axis=1)
    crs = _cross_prompt(qm_ref[0].astype(F32), mkT_ref[0].astype(BF16), mvT_ref[0].astype(BF16))
    xo_ref[0] = _finish_prompt(x_ref[0], tok, crs, sg_ref[0].astype(F32), wout_ref,
                               g_ref[...], b_ref[...], alpha)


def _fox_attn_prompt(q, kTb, vTb, d, dTb, x, qm, sg, mkT, mvT, wout, g, b, *, alpha):
    bsz, seq, d_model = x.shape
    _, nblk, nh, _, tk = kTb.shape
    tq = tk
    hp = dTb.shape[2]
    n_mem = mkT.shape[2]
    kern = functools.partial(_fox_attn_kernel, alpha=alpha, nh=nh)
    row = lambda bb, t: (bb, t, 0)
    per_b3 = lambda bb, t: (bb, 0, 0)
    per_b5 = lambda bb, t: (bb, 0, 0, 0, 0)
    return pl.pallas_call(
        kern,
        grid=(bsz, nblk),
        in_specs=[pl.BlockSpec((1, tq, nh * HEAD_DIM), row),
                  pl.BlockSpec((1, nblk, nh, HEAD_DIM, tk), per_b5, pipeline_mode=pl.Buffered(1)),
                  pl.BlockSpec((1, nblk, nh, HEAD_DIM, tk), per_b5, pipeline_mode=pl.Buffered(1)),
                  pl.BlockSpec((1, tq, LANES), row),
                  pl.BlockSpec((1, nblk, hp, 1, tk), per_b5, pipeline_mode=pl.Buffered(1)),
                  pl.BlockSpec((1, tq, d_model), row),
                  pl.BlockSpec((1, tq, X_WIDTH), row),
                  pl.BlockSpec((1, tq, d_model), row),
                  pl.BlockSpec((1, X_WIDTH, n_mem), per_b3),
                  pl.BlockSpec((1, X_WIDTH, n_mem), per_b3),
                  _const_spec(wout.shape), _const_spec(g.shape), _const_spec(b.shape)],
        out_specs=pl.BlockSpec((1, tq, d_model), row),
        out_shape=jax.ShapeDtypeStruct(x.shape, F32),
        scratch_shapes=[pltpu.VMEM((nh, tq, HEAD_DIM), BF16),
                        pltpu.VMEM((nh, tq, LANES), F32),
                        pltpu.VMEM((tq, LANES), F32),
                        pltpu.VMEM((tq, LANES), F32),
                        pltpu.VMEM((tq, HEAD_DIM), F32),
                        pltpu.VMEM((nh, tq, HEAD_DIM), F32)],
        compiler_params=pltpu.CompilerParams(dimension_semantics=("arbitrary", "arbitrary"),
                                             vmem_limit_bytes=VMEM_LIMIT),
        name="fox_attn_prompt",
    )(q, kTb, vTb, d, dTb, x, qm, sg, mkT, mvT, wout, g, b)


def _s_pre_conv_kernel(xT_ref, wT_ref, st0_ref, st1_ref, cw_ref,
                       tokT_ref, uT_ref, qmT_ref, sgT_ref, *, tw):
    xb = xT_ref[...].astype(BF16)
    bg = _dot(wT_ref[0:tw, :], xb)
    u = _dot(wT_ref[tw:2 * tw, :], xb) * _dot(wT_ref[2 * tw:3 * tw, :], xb)
    cw = cw_ref[...]
    y = cw[:, 0:1] * st0_ref[...] + cw[:, 1:2] * st1_ref[...] + cw[:, 2:3] * u
    tokT_ref[...] = bg * y
    uT_ref[...] = u
    qmT_ref[...] = _dot(wT_ref[3 * tw:3 * tw + X_WIDTH, :], xb) * QK_SCALE
    sgT_ref[...] = _silu(_dot(wT_ref[3 * tw + X_WIDTH:, :], xb))


def _s_pre_conv(xT, wT, st0T, st1T, cwT):
    d_model, nb = xT.shape
    tw = cwT.shape[0]
    f = lambda r: jax.ShapeDtypeStruct((r, nb), F32)
    return pl.pallas_call(
        functools.partial(_s_pre_conv_kernel, tw=tw),
        out_shape=[f(tw), f(tw), f(X_WIDTH), f(d_model)],
        compiler_params=pltpu.CompilerParams(vmem_limit_bytes=VMEM_LIMIT),
        name="sample_pre_conv",
    )(xT, wT, st0T, st1T, cwT)


def _s_pre_fox_kernel(xT_ref, wT_ref, bf_ref,
                      qT_ref, kT_ref, vT_ref, lfT_ref, slT_ref, qmT_ref, sgT_ref, *, tw, nh):
    xb = xT_ref[...].astype(BF16)
    nb = xb.shape[1]
    hp = bf_ref.shape[0]
    q = _dot(wT_ref[0:tw, :], xb) * QK_SCALE
    k = _dot(wT_ref[tw:2 * tw, :], xb)
    qT_ref[...] = q
    kT_ref[...] = k
    vT_ref[...] = _dot(wT_ref[2 * tw:3 * tw, :], xb)
    qmT_ref[...] = _dot(wT_ref[3 * tw:3 * tw + X_WIDTH, :], xb) * QK_SCALE
    o = 3 * tw + X_WIDTH
    d_model = xb.shape[0]
    sgT_ref[...] = _silu(_dot(wT_ref[o:o + d_model, :], xb))
    fT = _dot(wT_ref[o + d_model:, :], xb) + bf_ref[...]
    hrow = lax.broadcasted_iota(jnp.int32, (hp, nb), 0)
    lfT_ref[...] = jnp.where(hrow < nh, _log_sigmoid(fT), 0.0)
    qk = q * k
    rows = [jnp.sum(qk[h * HEAD_DIM:(h + 1) * HEAD_DIM], axis=0, keepdims=True) for h in range(nh)]
    rows.append(jnp.zeros((hp - nh, nb), F32))
    slT_ref[...] = jnp.concatenate(rows, axis=0)


def _s_pre_fox(xT, wT, bf, *, nh):
    d_model, nb = xT.shape
    tw = nh * HEAD_DIM
    hp = bf.shape[0]
    f = lambda r: jax.ShapeDtypeStruct((r, nb), F32)
    return pl.pallas_call(
        functools.partial(_s_pre_fox_kernel, tw=tw, nh=nh),
        out_shape=[f(tw), f(tw), f(tw), f(hp), f(hp), f(X_WIDTH), f(d_model)],
        compiler_params=pltpu.CompilerParams(vmem_limit_bytes=VMEM_LIMIT),
        name="sample_pre_fox",
    )(xT, wT, bf)


def _column(ref_or_val, onehot):
    return jnp.sum(jnp.where(onehot, ref_or_val, 0.0), axis=1, keepdims=True)


def _cross_sample_col(qc, mk, mv):
    outs = []
    for h in range(X_HEADS):
        s = jnp.sum(qc[h * HEAD_DIM:(h + 1) * HEAD_DIM] * mk[h], axis=0, keepdims=True)
        m = jnp.max(s, axis=1, keepdims=True)
        e = jnp.exp(s - m)
        p = e * (1.0 / jnp.sum(e, axis=1, keepdims=True))
        outs.append(jnp.sum(p * mv[h], axis=1, keepdims=True))
    return jnp.concatenate(outs, axis=0)


def _s_cross_kernel(qmT_ref, mk_ref, mv_ref, crsT_ref, *, bc):
    c = pl.program_id(0)

    @pl.when(c == 0)
    def _():
        crsT_ref[...] = jnp.zeros(crsT_ref.shape, F32)

    lane = lax.broadcasted_iota(jnp.int32, crsT_ref.shape, 1)
    qmT = qmT_ref[...]
    for bb in range(bc):
        onehot = lane == c * bc + bb
        col = _cross_sample_col(_column(qmT, onehot), mk_ref.at[bb], mv_ref.at[bb])
        crsT_ref[...] = jnp.where(onehot, col, crsT_ref[...])


def _s_cross(qmT, mk5, mv5, layer, *, bc):
    nb = qmT.shape[1]
    n_mem = mk5.shape[4]
    blk = pl.BlockSpec((None, bc, X_HEADS, HEAD_DIM, n_mem), lambda c: (layer, c, 0, 0, 0))
    return pl.pallas_call(
        functools.partial(_s_cross_kernel, bc=bc),
        grid=(nb // bc,),
        in_specs=[pl.BlockSpec(qmT.shape, lambda c: (0, 0)), blk, blk],
        out_specs=pl.BlockSpec(qmT.shape, lambda c: (0, 0)),
        out_shape=jax.ShapeDtypeStruct(qmT.shape, F32),
        compiler_params=pltpu.CompilerParams(dimension_semantics=("arbitrary",),
                                             vmem_limit_bytes=VMEM_LIMIT),
        name="sample_cross",
    )(qmT, mk5, mv5)


def _s_fox_kernel(pt_ref, qT_ref, vnT_ref, slT_ref, lfnT_ref, qmT_ref, mk_ref, mv_ref, *rest,
                  nh, n_pages):
    lf_pages = rest[0:n_pages]
    k_pages = rest[n_pages:2 * n_pages]
    v_pages = rest[2 * n_pages:3 * n_pages]
    tokT_ref, crsT_ref, l_scr, s_scr = rest[3 * n_pages:]
    b = pl.program_id(0)
    page = k_pages[0].shape[2]
    nrow = nh * n_pages

    @pl.when(b == 0)
    def _():
        tokT_ref[...] = jnp.zeros(tokT_ref.shape, F32)
        crsT_ref[...] = jnp.zeros(crsT_ref.shape, F32)

    def onehot(shape):
        return lax.broadcasted_iota(jnp.int32, shape, 1) == b

    qc = _column(qT_ref[...], onehot(qT_ref.shape))
    vnc = _column(vnT_ref[...], onehot(vnT_ref.shape))
    slc = _column(slT_ref[...], onehot(slT_ref.shape))
    lfn = _column(lfnT_ref[...], onehot(lfnT_ref.shape))

    for p in range(n_pages):
        for h in range(nh):
            l_scr[h * n_pages + p:h * n_pages + p + 1, :] = lf_pages[p][h]
    lfp = l_scr[...]
    r = lax.broadcasted_iota(jnp.int32, (page, page), 0)
    c = lax.broadcasted_iota(jnp.int32, (page, page), 1)
    after_in_page = jnp.where(r > c, 1.0, 0.0).astype(BF16)
    w3 = _dot(jnp.concatenate(_split3(lfp), axis=0), after_in_page)
    within = w3[0:nrow] + w3[nrow:2 * nrow] + w3[2 * nrow:3 * nrow]
    tot = jnp.broadcast_to(jnp.sum(lfp, axis=1, keepdims=True), (nrow, page))
    ru = lax.broadcasted_iota(jnp.int32, (nrow, nrow), 0)
    cu = lax.broadcasted_iota(jnp.int32, (nrow, nrow), 1)
    later_page = jnp.where((cu > ru) & (cu // n_pages == ru // n_pages), 1.0, 0.0).astype(BF16)
    a3 = _dot(later_page, jnp.concatenate(_split3(tot), axis=1))
    bias = within + a3[:, 0:page] + a3[:, page:2 * page] + a3[:, 2 * page:3 * page]

    for h in range(nh):
        qh = qc[h * HEAD_DIM:(h + 1) * HEAD_DIM]
        for p in range(n_pages):
            s_scr[h * n_pages + p:h * n_pages + p + 1, :] = jnp.sum(qh * k_pages[p][h], axis=0, keepdims=True)
    s_all = s_scr[...] + bias

    oh_tok = onehot((HEAD_DIM, tokT_ref.shape[1]))
    for h in range(nh):
        s_h = s_all[h * n_pages:(h + 1) * n_pages] + lfn[h:h + 1]
        sl = slc[h:h + 1]
        m = jnp.maximum(jnp.max(jnp.max(s_h, axis=1, keepdims=True), axis=0, keepdims=True), sl)
        p_h = jnp.exp(s_h - m)
        p_self = jnp.exp(sl - m)
        l = jnp.sum(jnp.sum(p_h, axis=1, keepdims=True), axis=0, keepdims=True) + p_self
        acc = p_h[0:1] * v_pages[0][h]
        for p in range(1, n_pages):
            acc = acc + p_h[p:p + 1] * v_pages[p][h]
        o = jnp.sum(acc, axis=1, keepdims=True) + p_self * vnc[h * HEAD_DIM:(h + 1) * HEAD_DIM]
        o = o / l
        sl_rows = slice(h * HEAD_DIM, (h + 1) * HEAD_DIM)
        tokT_ref[sl_rows, :] = jnp.where(oh_tok, o, tokT_ref[sl_rows, :])

    oh_x = onehot(crsT_ref.shape)
    col = _cross_sample_col(_column(qmT_ref[...], oh_x), mk_ref, mv_ref)
    crsT_ref[...] = jnp.where(oh_x, col, crsT_ref[...])


def _s_fox(page_table, qT, vnT, slT, lfnT, qmT, mk5, mv5, lf5, k5, v5, layer, fox_layer, *, nh):
    tw, nb = qT.shape
    n_pages = page_table.shape[1]
    page = k5.shape[4]
    n_mem = mk5.shape[4]
    hp = slT.shape[0]
    full = lambda a: pl.BlockSpec(a.shape, lambda b, pt: (0, 0))
    mem_blk = pl.BlockSpec((None, None, X_HEADS, HEAD_DIM, n_mem), lambda b, pt: (layer, b, 0, 0, 0))

    def lf_spec(p):
        return pl.BlockSpec((None, nh, None, 1, page), lambda b, pt: (fox_layer, 0, pt[b, p], 0, 0))

    def kv_spec(p):
        return pl.BlockSpec((None, None, nh, HEAD_DIM, page), lambda b, pt: (fox_layer, pt[b, p], 0, 0, 0))

    in_specs = ([full(qT), full(vnT), full(slT), full(lfnT), full(qmT), mem_blk, mem_blk]
                + [lf_spec(p) for p in range(n_pages)]
                + [kv_spec(p) for p in range(n_pages)]
                + [kv_spec(p) for p in range(n_pages)])
    grid_spec = pltpu.PrefetchScalarGridSpec(
        num_scalar_prefetch=1,
        grid=(nb,),
        in_specs=in_specs,
        out_specs=[pl.BlockSpec((tw, nb), lambda b, pt: (0, 0)),
                   pl.BlockSpec((X_WIDTH, nb), lambda b, pt: (0, 0))],
        scratch_shapes=[pltpu.VMEM((nh * n_pages, page), F32), pltpu.VMEM((nh * n_pages, page), F32)],
    )
    return pl.pallas_call(
        functools.partial(_s_fox_kernel, nh=nh, n_pages=n_pages),
        grid_spec=grid_spec,
        out_shape=[jax.ShapeDtypeStruct((tw, nb), F32), jax.ShapeDtypeStruct((X_WIDTH, nb), F32)],
        compiler_params=pltpu.CompilerParams(dimension_semantics=("arbitrary",),
                                             vmem_limit_bytes=VMEM_LIMIT),
        name="sample_fox_attn",
    )(page_table, qT, vnT, slT, lfnT, qmT, mk5, mv5, *([lf5] * n_pages), *([k5] * n_pages), *([v5] * n_pages))


def _s_post_kernel(xT_ref, tokT_ref, crsT_ref, sgT_ref, woT_ref, g_ref, b_ref, out_ref, *, alpha):
    tw = tokT_ref.shape[0]
    sg = sgT_ref[...]
    mt = (tokT_ref[...] * sg[0:tw]).astype(BF16)
    mc = (crsT_ref[...] * sg[tw:]).astype(BF16)
    y = alpha * xT_ref[...] + _dot(woT_ref[:, 0:tw], mt) + _dot(woT_ref[:, tw:], mc)
    out_ref[...] = _layer_norm(y, g_ref[...], b_ref[...], axis=0)


def _s_post(xT, tokT, crsT, sgT, woT, gcol, bcol, *, alpha):
    return pl.pallas_call(
        functools.partial(_s_post_kernel, alpha=alpha),
        out_shape=jax.ShapeDtypeStruct(xT.shape, F32),
        compiler_params=pltpu.CompilerParams(vmem_limit_bytes=VMEM_LIMIT),
        name="sample_post",
    )(xT, tokT, crsT, sgT, woT, gcol, bcol)


def kernel(x_prompt, x_sample, mem_prompt, cache_fox_k, cache_fox_v, cache_fox_logf, state_conv,
           cache_mem_k, cache_mem_v, page_table, w_in_conv, conv_w, w_in_fox, b_forget, w_mem_kv,
           w_out, ln_g, ln_b):
    depth = w_out.shape[0]
    alpha = float((2 * depth) ** 0.25)
    bsz, seq, d_model = x_prompt.shape
    nb = x_sample.shape[0]
    tw = conv_w.shape[2]
    nh = tw // HEAD_DIM
    hp = -(-nh // 8) * 8
    tm = min(512, seq)
    bc = min(8, nb)
    page = cache_fox_k.shape[2]

    mkT_all, mvT_all = _mem_kv(mem_prompt, jnp.swapaxes(w_mem_kv, 1, 2).astype(BF16))
    mk5 = jnp.transpose(cache_mem_k, (0, 1, 3, 4, 2))
    mv5 = jnp.transpose(cache_mem_v, (0, 1, 3, 4, 2))
    k5 = jnp.transpose(cache_fox_k, (0, 1, 3, 4, 2))
    v5 = jnp.transpose(cache_fox_v, (0, 1, 3, 4, 2))
    lf5 = jnp.transpose(cache_fox_logf, (0, 3, 1, 2))[:, :, :, None, :]

    xp = x_prompt
    xsT = x_sample.reshape(nb, d_model).T
    fk_p, fv_p, fl_p, fk_s, fv_s, fl_s, cv_p, cv_s = [], [], [], [], [], [], [], []

    for i in range(depth):
        j = i // 2
        woutb = w_out[i].astype(BF16)
        woutT = w_out[i].T.astype(BF16)
        g_row, b_row = ln_g[i][None, :], ln_b[i][None, :]
        g_col, b_col = ln_g[i][:, None], ln_b[i][:, None]
        if i % 2 == 0:
            w = w_in_conv[j]
            xp, st_p = _conv_layer_prompt(xp, w.astype(BF16), conv_w[j], mkT_all[i], mvT_all[i],
                                          woutb, g_row, b_row, alpha=alpha, tm=tm)
            cv_p.append(st_p)
            st = state_conv[j]
            tokT, uT, qmT, sgT = _s_pre_conv(xsT, w.T.astype(BF16), st[:, 0].T, st[:, 1].T, conv_w[j].T)
            cv_s.append(jnp.stack([st[:, 1], uT.T], axis=1))
            crsT = _s_cross(qmT, mk5, mv5, i, bc=bc)
        else:
            w = w_in_fox[j]
            o_f, o_qm, o_g = 3 * tw, 3 * tw + nh, 3 * tw + nh + X_WIDTH
            wq, wk, wv = w[:, 0:tw], w[:, tw:2 * tw], w[:, 2 * tw:3 * tw]
            wf, wqm, wg = w[:, o_f:o_qm], w[:, o_qm:o_g], w[:, o_g:]
            wfT = jnp.pad(wf.T, ((0, hp - nh), (0, 0)))
            bf = jnp.pad(b_forget[j], (0, hp - nh))[:, None]
            wn = jnp.concatenate([wq, wqm, wg], axis=1).astype(BF16)
            wkvT = jnp.concatenate([wk.T, wv.T], axis=0).astype(BF16)
            q, kT, vT, kTb, vTb, lfT, d, dTb, qm, sg = _fox_proj_prompt(
                xp, wn, wkvT, wfT.astype(BF16), bf, tm=tm, nh=nh)
            xp = _fox_attn_prompt(q, kTb, vTb, d, dTb, xp, qm, sg, mkT_all[i], mvT_all[i],
                                  woutb, g_row, b_row, alpha=alpha)
            fk_p.append(jnp.transpose(kT.reshape(bsz, nh, HEAD_DIM, seq), (0, 3, 1, 2)))
            fv_p.append(jnp.transpose(vT.reshape(bsz, nh, HEAD_DIM, seq), (0, 3, 1, 2)))
            fl_p.append(jnp.transpose(lfT[:, :nh, :], (0, 2, 1)))

            wT = jnp.concatenate([wq.T, wk.T, wv.T, wqm.T, wg.T, wfT], axis=0).astype(BF16)
            qT, knT, vnT, lfnT, slT, qmT, sgT = _s_pre_fox(xsT, wT, bf, nh=nh)
            tokT, crsT = _s_fox(page_table, qT, vnT, slT, lfnT, qmT, mk5, mv5, lf5, k5, v5, i, j, nh=nh)
            fk_s.append(knT.T.reshape(nb, 1, nh, HEAD_DIM))
            fv_s.append(vnT.T.reshape(nb, 1, nh, HEAD_DIM))
            fl_s.append(lfnT[:nh].T.reshape(nb, 1, nh))
        xsT = _s_post(xsT, tokT, crsT, sgT, woutT, g_col, b_col, alpha=alpha)

    n_mem = mem_prompt.shape[1]
    mk_out = jnp.transpose(mkT_all.reshape(depth, bsz, X_HEADS, HEAD_DIM, n_mem), (0, 1, 4, 2, 3))
    mv_out = jnp.transpose(mvT_all.reshape(depth, bsz, X_HEADS, HEAD_DIM, n_mem), (0, 1, 4, 2, 3))
    return (xp, xsT.T.reshape(nb, 1, d_model),
            jnp.stack(fk_p), jnp.stack(fv_p), jnp.stack(fl_p),
            jnp.stack(fk_s), jnp.stack(fv_s), jnp.stack(fl_s),
            jnp.stack(cv_p), jnp.stack(cv_s), mk_out, mv_out)
```

```python
import functools

import jax
import jax.numpy as jnp
from jax import lax
from jax.experimental import pallas as pl
from jax.experimental.pallas import tpu as pltpu

F32 = jnp.float32
BF16 = jnp.bfloat16

HEAD_DIM = 64
X_HEADS = 4
X_WIDTH = X_HEADS * HEAD_DIM
CONV_WIDTH = 3
LN_EPS = 1e-5
QK_SCALE = HEAD_DIM ** -0.5
LOG2E = 1.4426950408889634
SKIP_LOG2 = 150.0
NORM_SLACK = 1.02
MAX_HEAD_GROUP = 6
LANES = 128
VMEM_LIMIT = 56 * 1024 * 1024

_NT = (((1,), (1,)), ((), ()))


def _dot(a, b):
    return jnp.dot(a, b, preferred_element_type=F32)


def _dot_nt(a, b):
    return lax.dot_general(a, b, _NT, preferred_element_type=F32)


def _silu(g):
    return g * jax.nn.sigmoid(g)


def _log_sigmoid(x):
    return jnp.minimum(x, 0.0) - jnp.log1p(jnp.exp(-jnp.abs(x)))


def _layer_norm(y, g, b, axis):
    mu = jnp.mean(y, axis=axis, keepdims=True)
    yc = y - mu
    var = jnp.mean(yc * yc, axis=axis, keepdims=True)
    return yc * lax.rsqrt(var + LN_EPS) * g + b


def _split3(x):
    hi = x.astype(BF16)
    r1 = x - hi.astype(F32)
    mid = r1.astype(BF16)
    lo = (r1 - mid.astype(F32)).astype(BF16)
    return hi, mid, lo


def _const_spec(shape):
    nd = len(shape)
    return pl.BlockSpec(shape, lambda *_: (0,) * nd, pipeline_mode=pl.Buffered(1))


def _mem_kv_kernel(mem_ref, wT_ref, kT_ref, vT_ref):
    memb = mem_ref[0].astype(BF16)
    kvT = _dot_nt(wT_ref[0], memb)
    kT_ref[0, 0] = kvT[:X_WIDTH]
    vT_ref[0, 0] = kvT[X_WIDTH:]


def _mem_kv(mem_prompt, w_mem_kvT):
    depth = w_mem_kvT.shape[0]
    bsz, n_mem, d_model = mem_prompt.shape
    out = jax.ShapeDtypeStruct((depth, bsz, X_WIDTH, n_mem), F32)
    return pl.pallas_call(
        _mem_kv_kernel,
        grid=(depth, bsz),
        in_specs=[pl.BlockSpec((1, n_mem, d_model), lambda i, b: (b, 0, 0)),
                  pl.BlockSpec((1, 2 * X_WIDTH, d_model), lambda i, b: (i, 0, 0))],
        out_specs=[pl.BlockSpec((1, 1, X_WIDTH, n_mem), lambda i, b: (i, b, 0, 0))] * 2,
        out_shape=[out, out],
        compiler_params=pltpu.CompilerParams(dimension_semantics=("arbitrary", "arbitrary")),
        name="mem_kv",
    )(mem_prompt, w_mem_kvT)


def _cross_prompt(qm, kT, vT):
    lane = lax.broadcasted_iota(jnp.int32, qm.shape, 1)
    vrow = lax.broadcasted_iota(jnp.int32, vT.shape, 0)
    out = None
    for h in range(X_HEADS):
        lo, hi = h * HEAD_DIM, (h + 1) * HEAD_DIM
        qh = jnp.where((lane >= lo) & (lane < hi), qm, 0.0).astype(BF16)
        s = _dot(qh, kT)
        m = jnp.max(s, axis=-1, keepdims=True)
        e = jnp.exp(s - m)
        p = e * (1.0 / jnp.sum(e, axis=-1, keepdims=True))
        vh = jnp.where((vrow >= lo) & (vrow < hi), vT, jnp.zeros_like(vT))
        o = _dot_nt(p.astype(BF16), vh)
        out = o if out is None else out + o
    return out


def _finish_prompt(x, tok, crs, sg, wout_ref, g, b, alpha):
    tw = tok.shape[1]
    mt = (tok * sg[:, :tw]).astype(BF16)
    mc = (crs * sg[:, tw:]).astype(BF16)
    y = alpha * x + _dot(mt, wout_ref[0:tw, :]) + _dot(mc, wout_ref[tw:, :])
    return _layer_norm(y, g, b, axis=-1)


def _conv_layer_kernel(x_ref, win_ref, cw_ref, mkT_ref, mvT_ref, wout_ref, g_ref, b_ref,
                       xo_ref, st_ref, ubuf, *, alpha, tw):
    t = pl.program_id(1)
    tm = x_ref.shape[1]

    @pl.when(t == 0)
    def _():
        ubuf[0:8, :] = jnp.zeros((8, tw), F32)

    x = x_ref[0]
    xb = x.astype(BF16)
    bg = _dot(xb, win_ref[:, 0:tw])
    u = _dot(xb, win_ref[:, tw:2 * tw]) * _dot(xb, win_ref[:, 2 * tw:3 * tw])
    ubuf[8:8 + tm, :] = u
    cw = cw_ref[...]
    y = cw[0:1] * ubuf[6:6 + tm, :] + cw[1:2] * ubuf[7:7 + tm, :] + cw[2:3] * u
    tok = bg * y
    last = ubuf[tm + 6:tm + 8, :]
    ubuf[6:8, :] = last
    st_ref[0] = last

    qm = _dot(xb, win_ref[:, 3 * tw:3 * tw + X_WIDTH]) * QK_SCALE
    gate = _dot(xb, win_ref[:, 3 * tw + X_WIDTH:])
    crs = _cross_prompt(qm, mkT_ref[0].astype(BF16), mvT_ref[0].astype(BF16))
    xo_ref[0] = _finish_prompt(x, tok, crs, _silu(gate), wout_ref, g_ref[...], b_ref[...], alpha)


def _conv_layer_prompt(x, win, cw, mkT, mvT, wout, g, b, *, alpha, tm):
    bsz, seq, d_model = x.shape
    tw = cw.shape[1]
    n_mem = mkT.shape[2]
    kern = functools.partial(_conv_layer_kernel, alpha=alpha, tw=tw)
    return pl.pallas_call(
        kern,
        grid=(bsz, seq // tm),
        in_specs=[pl.BlockSpec((1, tm, d_model), lambda bb, t: (bb, t, 0)),
                  _const_spec(win.shape),
                  _const_spec(cw.shape),
                  pl.BlockSpec((1, X_WIDTH, n_mem), lambda bb, t: (bb, 0, 0)),
                  pl.BlockSpec((1, X_WIDTH, n_mem), lambda bb, t: (bb, 0, 0)),
                  _const_spec(wout.shape),
                  _const_spec(g.shape),
                  _const_spec(b.shape)],
        out_specs=[pl.BlockSpec((1, tm, d_model), lambda bb, t: (bb, t, 0)),
                   pl.BlockSpec((1, CONV_WIDTH - 1, tw), lambda bb, t: (bb, 0, 0))],
        out_shape=[jax.ShapeDtypeStruct(x.shape, F32),
                   jax.ShapeDtypeStruct((bsz, CONV_WIDTH - 1, tw), F32)],
        scratch_shapes=[pltpu.VMEM((tm + 8, tw), F32)],
        compiler_params=pltpu.CompilerParams(dimension_semantics=("arbitrary", "arbitrary"),
                                             vmem_limit_bytes=VMEM_LIMIT),
        name="conv_layer_prompt",
    )(x, win, cw, mkT, mvT, wout, g, b)


def _fox_proj_kernel(x_ref, wn_ref, wkvT_ref, wfT_ref, bf_ref, *rest, tw, nh, tb, n_alias):
    (q_ref, kT_ref, vT_ref, kTb_ref, vTb_ref, lfT_ref, d_ref, dTb_ref, jlo_ref, qm_ref, sg_ref,
     carry_ref, tri_ref, hk_ref, hd_ref) = rest[n_alias:]
    t = pl.program_id(1)
    tm = x_ref.shape[1]
    hp = wfT_ref.shape[0]
    nsub = tm // tb

    @pl.when(t == 0)
    def _():
        carry_ref[...] = jnp.zeros(carry_ref.shape, F32)
        hk_ref[...] = jnp.zeros(hk_ref.shape, F32)
        hd_ref[...] = jnp.zeros(hd_ref.shape, F32)
        r = lax.broadcasted_iota(jnp.int32, (tm, tm), 0)
        c = lax.broadcasted_iota(jnp.int32, (tm, tm), 1)
        tri_ref[...] = jnp.where(r <= c, 1.0, 0.0).astype(BF16)

    xb = x_ref[0].astype(BF16)
    q2 = (_dot(xb, wn_ref[:, 0:tw]) * (QK_SCALE * LOG2E)).astype(BF16)
    q_ref[0] = q2
    qm_ref[0] = (_dot(xb, wn_ref[:, tw:tw + X_WIDTH]) * QK_SCALE).astype(BF16)
    sg_ref[0] = _silu(_dot(xb, wn_ref[:, tw + X_WIDTH:])).astype(BF16)

    kT = _dot_nt(wkvT_ref[0:tw, :], xb)
    vT = _dot_nt(wkvT_ref[tw:2 * tw, :], xb)
    kT_ref[0] = kT
    vT_ref[0] = vT
    kTb = kT.astype(BF16)
    vTb = vT.astype(BF16)
    for u in range(nsub):
        for h in range(nh):
            kTb_ref[0, u, h] = kTb[h * HEAD_DIM:(h + 1) * HEAD_DIM, u * tb:(u + 1) * tb]
            vTb_ref[0, u, h] = vTb[h * HEAD_DIM:(h + 1) * HEAD_DIM, u * tb:(u + 1) * tb]

    fT = _dot_nt(wfT_ref[...], xb) + bf_ref[...]
    hrow = lax.broadcasted_iota(jnp.int32, (hp, tm), 0)
    lf = jnp.where(hrow < nh, _log_sigmoid(fT), 0.0)
    lfT_ref[0] = lf

    cs3 = _dot(jnp.concatenate(_split3(lf), axis=0), tri_ref[...])
    cs = cs3[0:hp] + cs3[hp:2 * hp] + cs3[2 * hp:3 * hp] + carry_ref[:, 0:1]
    carry_ref[...] = jnp.broadcast_to(cs[:, tm - 1:tm], carry_ref.shape)
    cs2 = cs * LOG2E
    for u in range(nsub):
        for h in range(hp):
            dTb_ref[0, u, h] = cs2[h:h + 1, u * tb:(u + 1) * tb]
    d_pad = jnp.concatenate([cs2, jnp.zeros((LANES - hp, tm), F32)], axis=0)
    d_ref[0] = d_pad.T

    head_of_col = lax.broadcasted_iota(jnp.int32, (LANES, tw), 1) // HEAD_DIM
    sel = jnp.where(head_of_col == lax.broadcasted_iota(jnp.int32, (LANES, tw), 0), 1.0, 0.0).astype(BF16)
    q2f = q2.astype(F32)
    nq = _dot_nt(sel, (q2f * q2f).astype(BF16))
    kf = kTb.astype(F32)
    ksq = kf * kf
    nk = jnp.concatenate([jnp.sum(ksq[h * HEAD_DIM:(h + 1) * HEAD_DIM], axis=0, keepdims=True)
                          for h in range(nh)] + [jnp.zeros((hp - nh, tm), F32)], axis=0)
    lane = lax.broadcasted_iota(jnp.int32, (hp, LANES), 1)
    lane_f = lane.astype(F32)
    for u in range(nsub):
        i = t * nsub + u
        qmax = jnp.sqrt(jnp.max(nq[0:hp, u * tb:(u + 1) * tb], axis=1, keepdims=True))
        kmax = jnp.sqrt(jnp.max(nk[:, u * tb:(u + 1) * tb], axis=1, keepdims=True))
        hk = jnp.where(lane == i, kmax, hk_ref[...])
        hd = jnp.where(lane == i, cs2[:, (u + 1) * tb - 1:(u + 1) * tb], hd_ref[...])
        hk_ref[...] = hk
        hd_ref[...] = hd
        gap = qmax * (hk + kmax) * NORM_SLACK + cs2[:, u * tb:u * tb + 1] - hd
        alive = (gap >= -SKIP_LOG2) & (lane < i)
        i_f = i.astype(F32)
        jlo_ref[0, u] = jnp.min(jnp.where(alive, lane_f, i_f), axis=1, keepdims=True).astype(jnp.int32)


def _fox_proj_prompt(x, wn, wkvT, wfT, bf, kv_prev, *, layer, n_layers, tm, tb, nh):
    bsz, seq, d_model = x.shape
    tw = nh * HEAD_DIM
    hp = wfT.shape[0]
    nblk = seq // tb
    nsub = tm // tb
    assert nblk <= LANES
    aliased = [] if kv_prev is None else list(kv_prev)
    kern = functools.partial(_fox_proj_kernel, tw=tw, nh=nh, tb=tb, n_alias=len(aliased))
    row = lambda bb, t: (bb, t, 0)
    colT = lambda bb, t: (bb, 0, t)
    blocked = lambda bb, t: (bb, t, 0, 0, 0)
    kv_all = jax.ShapeDtypeStruct((n_layers, bsz, tw, seq), F32)
    kv_spec = pl.BlockSpec((None, 1, tw, tm), lambda bb, t: (layer, bb, 0, t))
    out_shape = [
        jax.ShapeDtypeStruct((bsz, seq, tw), BF16),
        kv_all,
        kv_all,
        jax.ShapeDtypeStruct((bsz, nblk, nh, HEAD_DIM, tb), BF16),
        jax.ShapeDtypeStruct((bsz, nblk, nh, HEAD_DIM, tb), BF16),
        jax.ShapeDtypeStruct((bsz, hp, seq), F32),
        jax.ShapeDtypeStruct((bsz, seq, LANES), F32),
        jax.ShapeDtypeStruct((bsz, nblk, hp, 1, tb), F32),
        jax.ShapeDtypeStruct((bsz, nblk, hp, 1), jnp.int32),
        jax.ShapeDtypeStruct((bsz, seq, X_WIDTH), BF16),
        jax.ShapeDtypeStruct((bsz, seq, d_model), BF16),
    ]
    out_specs = [
        pl.BlockSpec((1, tm, tw), row),
        kv_spec,
        kv_spec,
        pl.BlockSpec((1, nsub, nh, HEAD_DIM, tb), blocked),
        pl.BlockSpec((1, nsub, nh, HEAD_DIM, tb), blocked),
        pl.BlockSpec((1, hp, tm), colT),
        pl.BlockSpec((1, tm, LANES), row),
        pl.BlockSpec((1, nsub, hp, 1, tb), blocked),
        pl.BlockSpec((1, nsub, hp, 1), lambda bb, t: (bb, t, 0, 0)),
        pl.BlockSpec((1, tm, X_WIDTH), row),
        pl.BlockSpec((1, tm, d_model), row),
    ]
    n_in = 5
    return pl.pallas_call(
        kern,
        grid=(bsz, seq // tm),
        in_specs=[pl.BlockSpec((1, tm, d_model), row),
                  _const_spec(wn.shape), _const_spec(wkvT.shape), _const_spec(wfT.shape),
                  _const_spec(bf.shape)] + [pl.BlockSpec(memory_space=pl.ANY)] * len(aliased),
        out_specs=out_specs,
        out_shape=out_shape,
        input_output_aliases={n_in + k: 1 + k for k in range(len(aliased))},
        scratch_shapes=[pltpu.VMEM((hp, LANES), F32), pltpu.VMEM((tm, tm), BF16),
                        pltpu.VMEM((hp, LANES), F32), pltpu.VMEM((hp, LANES), F32)],
        compiler_params=pltpu.CompilerParams(dimension_semantics=("arbitrary", "arbitrary"),
                                             vmem_limit_bytes=VMEM_LIMIT),
        name="fox_proj_prompt",
    )(x, wn, wkvT, wfT, bf, *aliased)


def _fox_attn_kernel(jlo_ref, q_ref, kTb_ref, vTb_ref, d_ref, dTb_ref, x_ref, qm_ref, sg_ref, mkT_ref,
                     mvT_ref, wout_ref, g_ref, b_ref, xo_ref,
                     qh_scr, dq_scr, m_scr, l_scr, acc_scr, tok_scr, *, alpha, nh, hp, group):
    bb = pl.program_id(0)
    i = pl.program_id(1)
    nblk = pl.num_programs(1)
    tq = q_ref.shape[1]
    tk = kTb_ref.shape[4]
    reps = tk // LANES

    for h in range(nh):
        qh_scr[h] = q_ref[0, :, h * HEAD_DIM:(h + 1) * HEAD_DIM]
        dq_scr[h] = jnp.broadcast_to(d_ref[0, :, h:h + 1], (tq, LANES))

    row = lax.broadcasted_iota(jnp.int32, (tq, tk), 0)
    col = lax.broadcasted_iota(jnp.int32, (tq, tk), 1)
    causal = col <= row

    def group_body(g, carry):
        heads = [g * group + k for k in range(group)]
        m_scr[...] = jnp.full(m_scr.shape, -jnp.inf, F32)
        l_scr[...] = jnp.zeros(l_scr.shape, F32)
        acc_scr[...] = jnp.zeros(acc_scr.shape, F32)
        qs = [qh_scr[h] for h in heads]
        dqts = [jnp.concatenate([dq_scr[h]] * reps, axis=1) for h in heads]

        def step(j, masked):
            for k, h in enumerate(heads):
                s = _dot(qs[k], kTb_ref[0, j, h])
                s = s + dqts[k] - dTb_ref[0, j, h]
                if masked:
                    s = jnp.where(causal, s, -jnp.inf)
                m_old = m_scr[k]
                m_new = jnp.maximum(m_old, jnp.max(s, axis=1, keepdims=True))
                a = jnp.exp2(m_old - m_new)
                p = jnp.exp2(s - jnp.concatenate([m_new] * reps, axis=1))
                l_scr[k] = a * l_scr[k] + jnp.sum(p, axis=1, keepdims=True)
                m_scr[k] = m_new
                pv = _dot_nt(p.astype(BF16), vTb_ref[0, j, h])
                acc_scr[k] = a[:, 0:HEAD_DIM] * acc_scr[k] + pv

        def loop_body(j, c):
            step(j, False)
            return c

        base = (bb * nblk + i) * hp
        j0 = jlo_ref[base + heads[0]]
        for h in heads[1:]:
            j0 = jnp.minimum(j0, jlo_ref[base + h])
        lax.fori_loop(j0, i, loop_body, 0)
        step(i, True)
        for k, h in enumerate(heads):
            tok_scr[h] = acc_scr[k] / l_scr[k][:, 0:HEAD_DIM]
        return carry

    lax.fori_loop(0, nh // group, group_body, 0)

    tok = jnp.concatenate([tok_scr[h] for h in range(nh)], axis=1)
    crs = _cross_prompt(qm_ref[0].astype(F32), mkT_ref[0].astype(BF16), mvT_ref[0].astype(BF16))
    xo_ref[0] = _finish_prompt(x_ref[0], tok, crs, sg_ref[0].astype(F32), wout_ref,
                               g_ref[...], b_ref[...], alpha)


def _fox_attn_prompt(jlo, q, kTb, vTb, d, dTb, x, qm, sg, mkT, mvT, wout, g, b, *, alpha):
    bsz, seq, d_model = x.shape
    _, nblk, nh, _, tk = kTb.shape
    tq = tk
    hp = dTb.shape[2]
    n_mem = mkT.shape[2]
    group = max(k for k in range(1, MAX_HEAD_GROUP + 1) if nh % k == 0)
    kern = functools.partial(_fox_attn_kernel, alpha=alpha, nh=nh, hp=hp, group=group)
    row = lambda bb, t, jl: (bb, t, 0)
    per_b3 = lambda bb, t, jl: (bb, 0, 0)
    per_b5 = lambda bb, t, jl: (bb, 0, 0, 0, 0)

    def const(a):
        nd = a.ndim
        return pl.BlockSpec(a.shape, lambda bb, t, jl: (0,) * nd, pipeline_mode=pl.Buffered(1))

    grid_spec = pltpu.PrefetchScalarGridSpec(
        num_scalar_prefetch=1,
        grid=(bsz, nblk),
        in_specs=[pl.BlockSpec((1, tq, nh * HEAD_DIM), row),
                  pl.BlockSpec((1, nblk, nh, HEAD_DIM, tk), per_b5, pipeline_mode=pl.Buffered(1)),
                  pl.BlockSpec((1, nblk, nh, HEAD_DIM, tk), per_b5, pipeline_mode=pl.Buffered(1)),
                  pl.BlockSpec((1, tq, LANES), row),
                  pl.BlockSpec((1, nblk, hp, 1, tk), per_b5, pipeline_mode=pl.Buffered(1)),
                  pl.BlockSpec((1, tq, d_model), row),
                  pl.BlockSpec((1, tq, X_WIDTH), row),
                  pl.BlockSpec((1, tq, d_model), row),
                  pl.BlockSpec((1, X_WIDTH, n_mem), per_b3),
                  pl.BlockSpec((1, X_WIDTH, n_mem), per_b3),
                  const(wout), const(g), const(b)],
        out_specs=pl.BlockSpec((1, tq, d_model), row),
        scratch_shapes=[pltpu.VMEM((nh, tq, HEAD_DIM), BF16),
                        pltpu.VMEM((nh, tq, LANES), F32),
                        pltpu.VMEM((group, tq, LANES), F32),
                        pltpu.VMEM((group, tq, LANES), F32),
                        pltpu.VMEM((group, tq, HEAD_DIM), F32),
                        pltpu.VMEM((nh, tq, HEAD_DIM), F32)],
    )
    return pl.pallas_call(
        kern,
        grid_spec=grid_spec,
        out_shape=jax.ShapeDtypeStruct(x.shape, F32),
        compiler_params=pltpu.CompilerParams(dimension_semantics=("arbitrary", "arbitrary"),
                                             vmem_limit_bytes=VMEM_LIMIT),
        name="fox_attn_prompt",
    )(jlo.reshape(-1), q, kTb, vTb, d, dTb, x, qm, sg, mkT, mvT, wout, g, b)


def _s_pre_conv_kernel(xT_ref, wT_ref, st0_ref, st1_ref, cw_ref,
                       tokT_ref, uT_ref, qmT_ref, sgT_ref, *, tw):
    xb = xT_ref[...].astype(BF16)
    bg = _dot(wT_ref[0:tw, :], xb)
    u = _dot(wT_ref[tw:2 * tw, :], xb) * _dot(wT_ref[2 * tw:3 * tw, :], xb)
    cw = cw_ref[...]
    y = cw[:, 0:1] * st0_ref[...] + cw[:, 1:2] * st1_ref[...] + cw[:, 2:3] * u
    tokT_ref[...] = bg * y
    uT_ref[...] = u
    qmT_ref[...] = _dot(wT_ref[3 * tw:3 * tw + X_WIDTH, :], xb) * QK_SCALE
    sgT_ref[...] = _silu(_dot(wT_ref[3 * tw + X_WIDTH:, :], xb))


def _s_pre_conv(xT, wT, st0T, st1T, cwT):
    d_model, nb = xT.shape
    tw = cwT.shape[0]
    f = lambda r: jax.ShapeDtypeStruct((r, nb), F32)
    return pl.pallas_call(
        functools.partial(_s_pre_conv_kernel, tw=tw),
        out_shape=[f(tw), f(tw), f(X_WIDTH), f(d_model)],
        compiler_params=pltpu.CompilerParams(vmem_limit_bytes=VMEM_LIMIT),
        name="sample_pre_conv",
    )(xT, wT, st0T, st1T, cwT)


def _s_pre_fox_kernel(xT_ref, wT_ref, bf_ref,
                      qT_ref, kT_ref, vT_ref, lfT_ref, slT_ref, qmT_ref, sgT_ref, *, tw, nh):
    xb = xT_ref[...].astype(BF16)
    nb = xb.shape[1]
    hp = bf_ref.shape[0]
    q = _dot(wT_ref[0:tw, :], xb) * QK_SCALE
    k = _dot(wT_ref[tw:2 * tw, :], xb)
    qT_ref[...] = q
    kT_ref[...] = k
    vT_ref[...] = _dot(wT_ref[2 * tw:3 * tw, :], xb)
    qmT_ref[...] = _dot(wT_ref[3 * tw:3 * tw + X_WIDTH, :], xb) * QK_SCALE
    o = 3 * tw + X_WIDTH
    d_model = xb.shape[0]
    sgT_ref[...] = _silu(_dot(wT_ref[o:o + d_model, :], xb))
    fT = _dot(wT_ref[o + d_model:, :], xb) + bf_ref[...]
    hrow = lax.broadcasted_iota(jnp.int32, (hp, nb), 0)
    lfT_ref[...] = jnp.where(hrow < nh, _log_sigmoid(fT), 0.0)
    qk = q * k
    rows = [jnp.sum(qk[h * HEAD_DIM:(h + 1) * HEAD_DIM], axis=0, keepdims=True) for h in range(nh)]
    rows.append(jnp.zeros((hp - nh, nb), F32))
    slT_ref[...] = jnp.concatenate(rows, axis=0)


def _s_pre_fox(xT, wT, bf, *, nh):
    d_model, nb = xT.shape
    tw = nh * HEAD_DIM
    hp = bf.shape[0]
    f = lambda r: jax.ShapeDtypeStruct((r, nb), F32)
    return pl.pallas_call(
        functools.partial(_s_pre_fox_kernel, tw=tw, nh=nh),
        out_shape=[f(tw), f(tw), f(tw), f(hp), f(hp), f(X_WIDTH), f(d_model)],
        compiler_params=pltpu.CompilerParams(vmem_limit_bytes=VMEM_LIMIT),
        name="sample_pre_fox",
    )(xT, wT, bf)


def _column(ref_or_val, onehot):
    return jnp.sum(jnp.where(onehot, ref_or_val, 0.0), axis=1, keepdims=True)


def _cross_sample_col(qc, mk, mv):
    outs = []
    for h in range(X_HEADS):
        s = jnp.sum(qc[h * HEAD_DIM:(h + 1) * HEAD_DIM] * mk[h], axis=0, keepdims=True)
        m = jnp.max(s, axis=1, keepdims=True)
        e = jnp.exp(s - m)
        p = e * (1.0 / jnp.sum(e, axis=1, keepdims=True))
        outs.append(jnp.sum(p * mv[h], axis=1, keepdims=True))
    return jnp.concatenate(outs, axis=0)


def _s_cross_kernel(qmT_ref, mk_ref, mv_ref, crsT_ref, *, bc):
    c = pl.program_id(0)

    @pl.when(c == 0)
    def _():
        crsT_ref[...] = jnp.zeros(crsT_ref.shape, F32)

    lane = lax.broadcasted_iota(jnp.int32, crsT_ref.shape, 1)
    qmT = qmT_ref[...]
    for bb in range(bc):
        onehot = lane == c * bc + bb
        col = _cross_sample_col(_column(qmT, onehot), mk_ref.at[bb], mv_ref.at[bb])
        crsT_ref[...] = jnp.where(onehot, col, crsT_ref[...])


def _s_cross(qmT, mk5, mv5, layer, *, bc):
    nb = qmT.shape[1]
    n_mem = mk5.shape[4]
    blk = pl.BlockSpec((None, bc, X_HEADS, HEAD_DIM, n_mem), lambda c: (layer, c, 0, 0, 0))
    return pl.pallas_call(
        functools.partial(_s_cross_kernel, bc=bc),
        grid=(nb // bc,),
        in_specs=[pl.BlockSpec(qmT.shape, lambda c: (0, 0)), blk, blk],
        out_specs=pl.BlockSpec(qmT.shape, lambda c: (0, 0)),
        out_shape=jax.ShapeDtypeStruct(qmT.shape, F32),
        compiler_params=pltpu.CompilerParams(dimension_semantics=("arbitrary",),
                                             vmem_limit_bytes=VMEM_LIMIT),
        name="sample_cross",
    )(qmT, mk5, mv5)


def _s_fox_kernel(pt_ref, qT_ref, vnT_ref, slT_ref, lfnT_ref, qmT_ref, mk_ref, mv_ref, *rest,
                  nh, n_pages):
    lf_pages = rest[0:n_pages]
    k_pages = rest[n_pages:2 * n_pages]
    v_pages = rest[2 * n_pages:3 * n_pages]
    tokT_ref, crsT_ref, l_scr, s_scr = rest[3 * n_pages:]
    b = pl.program_id(0)
    page = k_pages[0].shape[2]
    nrow = nh * n_pages

    @pl.when(b == 0)
    def _():
        tokT_ref[...] = jnp.zeros(tokT_ref.shape, F32)
        crsT_ref[...] = jnp.zeros(crsT_ref.shape, F32)

    def onehot(shape):
        return lax.broadcasted_iota(jnp.int32, shape, 1) == b

    qc = _column(qT_ref[...], onehot(qT_ref.shape))
    vnc = _column(vnT_ref[...], onehot(vnT_ref.shape))
    slc = _column(slT_ref[...], onehot(slT_ref.shape))
    lfn = _column(lfnT_ref[...], onehot(lfnT_ref.shape))

    for p in range(n_pages):
        for h in range(nh):
            l_scr[h * n_pages + p:h * n_pages + p + 1, :] = lf_pages[p][h]
    lfp = l_scr[...]
    r = lax.broadcasted_iota(jnp.int32, (page, page), 0)
    c = lax.broadcasted_iota(jnp.int32, (page, page), 1)
    after_in_page = jnp.where(r > c, 1.0, 0.0).astype(BF16)
    w3 = _dot(jnp.concatenate(_split3(lfp), axis=0), after_in_page)
    within = w3[0:nrow] + w3[nrow:2 * nrow] + w3[2 * nrow:3 * nrow]
    tot = jnp.broadcast_to(jnp.sum(lfp, axis=1, keepdims=True), (nrow, page))
    ru = lax.broadcasted_iota(jnp.int32, (nrow, nrow), 0)
    cu = lax.broadcasted_iota(jnp.int32, (nrow, nrow), 1)
    later_page = jnp.where((cu > ru) & (cu // n_pages == ru // n_pages), 1.0, 0.0).astype(BF16)
    a3 = _dot(later_page, jnp.concatenate(_split3(tot), axis=1))
    bias = within + a3[:, 0:page] + a3[:, page:2 * page] + a3[:, 2 * page:3 * page]

    for h in range(nh):
        qh = qc[h * HEAD_DIM:(h + 1) * HEAD_DIM]
        for p in range(n_pages):
            s_scr[h * n_pages + p:h * n_pages + p + 1, :] = jnp.sum(qh * k_pages[p][h], axis=0, keepdims=True)
    s_all = s_scr[...] + bias

    oh_tok = onehot((HEAD_DIM, tokT_ref.shape[1]))
    for h in range(nh):
        s_h = s_all[h * n_pages:(h + 1) * n_pages] + lfn[h:h + 1]
        sl = slc[h:h + 1]
        m = jnp.maximum(jnp.max(jnp.max(s_h, axis=1, keepdims=True), axis=0, keepdims=True), sl)
        p_h = jnp.exp(s_h - m)
        p_self = jnp.exp(sl - m)
        l = jnp.sum(jnp.sum(p_h, axis=1, keepdims=True), axis=0, keepdims=True) + p_self
        acc = p_h[0:1] * v_pages[0][h]
        for p in range(1, n_pages):
            acc = acc + p_h[p:p + 1] * v_pages[p][h]
        o = jnp.sum(acc, axis=1, keepdims=True) + p_self * vnc[h * HEAD_DIM:(h + 1) * HEAD_DIM]
        o = o / l
        sl_rows = slice(h * HEAD_DIM, (h + 1) * HEAD_DIM)
        tokT_ref[sl_rows, :] = jnp.where(oh_tok, o, tokT_ref[sl_rows, :])

    oh_x = onehot(crsT_ref.shape)
    col = _cross_sample_col(_column(qmT_ref[...], oh_x), mk_ref, mv_ref)
    crsT_ref[...] = jnp.where(oh_x, col, crsT_ref[...])


def _s_fox(page_table, qT, vnT, slT, lfnT, qmT, mk5, mv5, lf5, k5, v5, layer, fox_layer, *, nh):
    tw, nb = qT.shape
    n_pages = page_table.shape[1]
    page = k5.shape[4]
    n_mem = mk5.shape[4]
    hp = slT.shape[0]
    full = lambda a: pl.BlockSpec(a.shape, lambda b, pt: (0, 0))
    mem_blk = pl.BlockSpec((None, None, X_HEADS, HEAD_DIM, n_mem), lambda b, pt: (layer, b, 0, 0, 0))

    def lf_spec(p):
        return pl.BlockSpec((None, nh, None, 1, page), lambda b, pt: (fox_layer, 0, pt[b, p], 0, 0))

    def kv_spec(p):
        return pl.BlockSpec((None, None, nh, HEAD_DIM, page), lambda b, pt: (fox_layer, pt[b, p], 0, 0, 0))

    in_specs = ([full(qT), full(vnT), full(slT), full(lfnT), full(qmT), mem_blk, mem_blk]
                + [lf_spec(p) for p in range(n_pages)]
                + [kv_spec(p) for p in range(n_pages)]
                + [kv_spec(p) for p in range(n_pages)])
    grid_spec = pltpu.PrefetchScalarGridSpec(
        num_scalar_prefetch=1,
        grid=(nb,),
        in_specs=in_specs,
        out_specs=[pl.BlockSpec((tw, nb), lambda b, pt: (0, 0)),
                   pl.BlockSpec((X_WIDTH, nb), lambda b, pt: (0, 0))],
        scratch_shapes=[pltpu.VMEM((nh * n_pages, page), F32), pltpu.VMEM((nh * n_pages, page), F32)],
    )
    return pl.pallas_call(
        functools.partial(_s_fox_kernel, nh=nh, n_pages=n_pages),
        grid_spec=grid_spec,
        out_shape=[jax.ShapeDtypeStruct((tw, nb), F32), jax.ShapeDtypeStruct((X_WIDTH, nb), F32)],
        compiler_params=pltpu.CompilerParams(dimension_semantics=("arbitrary",),
                                             vmem_limit_bytes=VMEM_LIMIT),
        name="sample_fox_attn",
    )(page_table, qT, vnT, slT, lfnT, qmT, mk5, mv5, *([lf5] * n_pages), *([k5] * n_pages), *([v5] * n_pages))


def _s_post_kernel(xT_ref, tokT_ref, crsT_ref, sgT_ref, woT_ref, g_ref, b_ref, out_ref, *, alpha):
    tw = tokT_ref.shape[0]
    sg = sgT_ref[...]
    mt = (tokT_ref[...] * sg[0:tw]).astype(BF16)
    mc = (crsT_ref[...] * sg[tw:]).astype(BF16)
    y = alpha * xT_ref[...] + _dot(woT_ref[:, 0:tw], mt) + _dot(woT_ref[:, tw:], mc)
    out_ref[...] = _layer_norm(y, g_ref[...], b_ref[...], axis=0)


def _s_post(xT, tokT, crsT, sgT, woT, gcol, bcol, *, alpha):
    return pl.pallas_call(
        functools.partial(_s_post_kernel, alpha=alpha),
        out_shape=jax.ShapeDtypeStruct(xT.shape, F32),
        compiler_params=pltpu.CompilerParams(vmem_limit_bytes=VMEM_LIMIT),
        name="sample_post",
    )(xT, tokT, crsT, sgT, woT, gcol, bcol)


def kernel(x_prompt, x_sample, mem_prompt, cache_fox_k, cache_fox_v, cache_fox_logf, state_conv,
           cache_mem_k, cache_mem_v, page_table, w_in_conv, conv_w, w_in_fox, b_forget, w_mem_kv,
           w_out, ln_g, ln_b):
    depth = w_out.shape[0]
    alpha = float((2 * depth) ** 0.25)
    bsz, seq, d_model = x_prompt.shape
    nb = x_sample.shape[0]
    tw = conv_w.shape[2]
    nh = tw // HEAD_DIM
    hp = -(-nh // 8) * 8
    tm = min(512, seq)
    tb = min(256, seq)
    bc = min(8, nb)
    kv_all = None

    mkT_all, mvT_all = _mem_kv(mem_prompt, jnp.swapaxes(w_mem_kv, 1, 2).astype(BF16))
    mk5 = jnp.transpose(cache_mem_k, (0, 1, 3, 4, 2))
    mv5 = jnp.transpose(cache_mem_v, (0, 1, 3, 4, 2))
    k5 = jnp.transpose(cache_fox_k, (0, 1, 3, 4, 2))
    v5 = jnp.transpose(cache_fox_v, (0, 1, 3, 4, 2))
    lf5 = jnp.transpose(cache_fox_logf, (0, 3, 1, 2))[:, :, :, None, :]

    xp = x_prompt
    xsT = x_sample.reshape(nb, d_model).T
    fl_p, fk_s, fv_s, fl_s, cv_p, cv_s = [], [], [], [], [], []

    for i in range(depth):
        j = i // 2
        woutb = w_out[i].astype(BF16)
        woutT = w_out[i].T.astype(BF16)
        g_row, b_row = ln_g[i][None, :], ln_b[i][None, :]
        g_col, b_col = ln_g[i][:, None], ln_b[i][:, None]
        if i % 2 == 0:
            w = w_in_conv[j]
            xp, st_p = _conv_layer_prompt(xp, w.astype(BF16), conv_w[j], mkT_all[i], mvT_all[i],
                                          woutb, g_row, b_row, alpha=alpha, tm=tm)
            cv_p.append(st_p)
            st = state_conv[j]
            tokT, uT, qmT, sgT = _s_pre_conv(xsT, w.T.astype(BF16), st[:, 0].T, st[:, 1].T, conv_w[j].T)
            cv_s.append(jnp.stack([st[:, 1], uT.T], axis=1))
            crsT = _s_cross(qmT, mk5, mv5, i, bc=bc)
        else:
            w = w_in_fox[j]
            o_f, o_qm, o_g = 3 * tw, 3 * tw + nh, 3 * tw + nh + X_WIDTH
            wq, wk, wv = w[:, 0:tw], w[:, tw:2 * tw], w[:, 2 * tw:3 * tw]
            wf, wqm, wg = w[:, o_f:o_qm], w[:, o_qm:o_g], w[:, o_g:]
            wfT = jnp.pad(wf.T, ((0, hp - nh), (0, 0)))
            bf = jnp.pad(b_forget[j], (0, hp - nh))[:, None]
            wn = jnp.concatenate([wq, wqm, wg], axis=1).astype(BF16)
            wkvT = jnp.concatenate([wk.T, wv.T], axis=0).astype(BF16)
            q, kT_all, vT_all, kTb, vTb, lfT, d, dTb, jlo, qm, sg = _fox_proj_prompt(
                xp, wn, wkvT, wfT.astype(BF16), bf, kv_all, layer=j, n_layers=depth // 2,
                tm=tm, tb=tb, nh=nh)
            kv_all = (kT_all, vT_all)
            xp = _fox_attn_prompt(jlo, q, kTb, vTb, d, dTb, xp, qm, sg, mkT_all[i], mvT_all[i],
                                  woutb, g_row, b_row, alpha=alpha)
            fl_p.append(jnp.transpose(lfT[:, :nh, :], (0, 2, 1)))

            wT = jnp.concatenate([wq.T, wk.T, wv.T, wqm.T, wg.T, wfT], axis=0).astype(BF16)
            qT, knT, vnT, lfnT, slT, qmT, sgT = _s_pre_fox(xsT, wT, bf, nh=nh)
            tokT, crsT = _s_fox(page_table, qT, vnT, slT, lfnT, qmT, mk5, mv5, lf5, k5, v5, i, j, nh=nh)
            fk_s.append(knT.T.reshape(nb, 1, nh, HEAD_DIM))
            fv_s.append(vnT.T.reshape(nb, 1, nh, HEAD_DIM))
            fl_s.append(lfnT[:nh].T.reshape(nb, 1, nh))
        xsT = _s_post(xsT, tokT, crsT, sgT, woutT, g_col, b_col, alpha=alpha)

    n_mem = mem_prompt.shape[1]
    mk_out = jnp.transpose(mkT_all.reshape(depth, bsz, X_HEADS, HEAD_DIM, n_mem), (0, 1, 4, 2, 3))
    mv_out = jnp.transpose(mvT_all.reshape(depth, bsz, X_HEADS, HEAD_DIM, n_mem), (0, 1, 4, 2, 3))
    n_fox = depth // 2
    fk_all = jnp.transpose(kv_all[0].reshape(n_fox, bsz, nh, HEAD_DIM, seq), (0, 1, 4, 2, 3))
    fv_all = jnp.transpose(kv_all[1].reshape(n_fox, bsz, nh, HEAD_DIM, seq), (0, 1, 4, 2, 3))
    return (xp, xsT.T.reshape(nb, 1, d_model),
            fk_all, fv_all, jnp.stack(fl_p),
            jnp.stack(fk_s), jnp.stack(fv_s), jnp.stack(fl_s),
            jnp.stack(cv_p), jnp.stack(cv_s), mk_out, mv_out)
```

```python
import functools

import jax
import jax.numpy as jnp
from jax import lax
from jax.experimental import pallas as pl
from jax.experimental.pallas import tpu as pltpu

F32 = jnp.float32
BF16 = jnp.bfloat16

HEAD_DIM = 64
X_HEADS = 4
X_WIDTH = X_HEADS * HEAD_DIM
CONV_WIDTH = 3
LN_EPS = 1e-5
QK_SCALE = HEAD_DIM ** -0.5
LOG2E = 1.4426950408889634
SKIP_LOG2 = 150.0
NORM_SLACK = 1.02
MAX_HEAD_GROUP = 6
LANES = 128
VMEM_LIMIT = 56 * 1024 * 1024

_NT = (((1,), (1,)), ((), ()))


def _dot(a, b):
    return jnp.dot(a, b, preferred_element_type=F32)


def _dot_nt(a, b):
    return lax.dot_general(a, b, _NT, preferred_element_type=F32)


def _silu(g):
    return g * jax.nn.sigmoid(g)


def _log_sigmoid(x):
    return jnp.minimum(x, 0.0) - jnp.log1p(jnp.exp(-jnp.abs(x)))


def _layer_norm(y, g, b, axis):
    mu = jnp.mean(y, axis=axis, keepdims=True)
    yc = y - mu
    var = jnp.mean(yc * yc, axis=axis, keepdims=True)
    return yc * lax.rsqrt(var + LN_EPS) * g + b


def _split3(x):
    hi = x.astype(BF16)
    r1 = x - hi.astype(F32)
    mid = r1.astype(BF16)
    lo = (r1 - mid.astype(F32)).astype(BF16)
    return hi, mid, lo


def _const_spec(shape):
    nd = len(shape)
    return pl.BlockSpec(shape, lambda *_: (0,) * nd, pipeline_mode=pl.Buffered(1))


def _mem_kv_kernel(mem_ref, wT_ref, kT_ref, vT_ref):
    memb = mem_ref[0].astype(BF16)
    kvT = _dot_nt(wT_ref[0], memb)
    kT_ref[0, 0] = kvT[:X_WIDTH]
    vT_ref[0, 0] = kvT[X_WIDTH:]


def _mem_kv(mem_prompt, w_mem_kvT):
    depth = w_mem_kvT.shape[0]
    bsz, n_mem, d_model = mem_prompt.shape
    out = jax.ShapeDtypeStruct((depth, bsz, X_WIDTH, n_mem), F32)
    return pl.pallas_call(
        _mem_kv_kernel,
        grid=(depth, bsz),
        in_specs=[pl.BlockSpec((1, n_mem, d_model), lambda i, b: (b, 0, 0)),
                  pl.BlockSpec((1, 2 * X_WIDTH, d_model), lambda i, b: (i, 0, 0))],
        out_specs=[pl.BlockSpec((1, 1, X_WIDTH, n_mem), lambda i, b: (i, b, 0, 0))] * 2,
        out_shape=[out, out],
        compiler_params=pltpu.CompilerParams(dimension_semantics=("arbitrary", "arbitrary")),
        name="mem_kv",
    )(mem_prompt, w_mem_kvT)


def _cross_prompt(qm, kT, vT):
    lane = lax.broadcasted_iota(jnp.int32, qm.shape, 1)
    vrow = lax.broadcasted_iota(jnp.int32, vT.shape, 0)
    out = None
    for h in range(X_HEADS):
        lo, hi = h * HEAD_DIM, (h + 1) * HEAD_DIM
        qh = jnp.where((lane >= lo) & (lane < hi), qm, 0.0).astype(BF16)
        s = _dot(qh, kT)
        m = jnp.max(s, axis=-1, keepdims=True)
        e = jnp.exp(s - m)
        p = e * (1.0 / jnp.sum(e, axis=-1, keepdims=True))
        vh = jnp.where((vrow >= lo) & (vrow < hi), vT, jnp.zeros_like(vT))
        o = _dot_nt(p.astype(BF16), vh)
        out = o if out is None else out + o
    return out


def _finish_prompt(x, tok, crs, sg, wout_ref, g, b, alpha):
    tw = tok.shape[1]
    mt = (tok * sg[:, :tw]).astype(BF16)
    mc = (crs * sg[:, tw:]).astype(BF16)
    y = alpha * x + _dot(mt, wout_ref[0:tw, :]) + _dot(mc, wout_ref[tw:, :])
    return _layer_norm(y, g, b, axis=-1)


def _conv_layer_kernel(x_ref, win_ref, cw_ref, mkT_ref, mvT_ref, wout_ref, g_ref, b_ref,
                       xo_ref, st_ref, ubuf, *, alpha, tw):
    t = pl.program_id(1)
    tm = x_ref.shape[1]

    @pl.when(t == 0)
    def _():
        ubuf[0:8, :] = jnp.zeros((8, tw), F32)

    x = x_ref[0]
    xb = x.astype(BF16)
    bg = _dot(xb, win_ref[:, 0:tw])
    u = _dot(xb, win_ref[:, tw:2 * tw]) * _dot(xb, win_ref[:, 2 * tw:3 * tw])
    ubuf[8:8 + tm, :] = u
    cw = cw_ref[...]
    y = cw[0:1] * ubuf[6:6 + tm, :] + cw[1:2] * ubuf[7:7 + tm, :] + cw[2:3] * u
    tok = bg * y
    last = ubuf[tm + 6:tm + 8, :]
    ubuf[6:8, :] = last
    st_ref[0] = last

    qm = _dot(xb, win_ref[:, 3 * tw:3 * tw + X_WIDTH]) * QK_SCALE
    gate = _dot(xb, win_ref[:, 3 * tw + X_WIDTH:])
    crs = _cross_prompt(qm, mkT_ref[0].astype(BF16), mvT_ref[0].astype(BF16))
    xo_ref[0] = _finish_prompt(x, tok, crs, _silu(gate), wout_ref, g_ref[...], b_ref[...], alpha)


def _conv_layer_prompt(x, win, cw, mkT, mvT, wout, g, b, *, alpha, tm):
    bsz, seq, d_model = x.shape
    tw = cw.shape[1]
    n_mem = mkT.shape[2]
    kern = functools.partial(_conv_layer_kernel, alpha=alpha, tw=tw)
    return pl.pallas_call(
        kern,
        grid=(bsz, seq // tm),
        in_specs=[pl.BlockSpec((1, tm, d_model), lambda bb, t: (bb, t, 0)),
                  _const_spec(win.shape),
                  _const_spec(cw.shape),
                  pl.BlockSpec((1, X_WIDTH, n_mem), lambda bb, t: (bb, 0, 0)),
                  pl.BlockSpec((1, X_WIDTH, n_mem), lambda bb, t: (bb, 0, 0)),
                  _const_spec(wout.shape),
                  _const_spec(g.shape),
                  _const_spec(b.shape)],
        out_specs=[pl.BlockSpec((1, tm, d_model), lambda bb, t: (bb, t, 0)),
                   pl.BlockSpec((1, CONV_WIDTH - 1, tw), lambda bb, t: (bb, 0, 0))],
        out_shape=[jax.ShapeDtypeStruct(x.shape, F32),
                   jax.ShapeDtypeStruct((bsz, CONV_WIDTH - 1, tw), F32)],
        scratch_shapes=[pltpu.VMEM((tm + 8, tw), F32)],
        compiler_params=pltpu.CompilerParams(dimension_semantics=("arbitrary", "arbitrary"),
                                             vmem_limit_bytes=VMEM_LIMIT),
        name="conv_layer_prompt",
    )(x, win, cw, mkT, mvT, wout, g, b)


def _fox_proj_kernel(x_ref, wn_ref, wkvT_ref, wfT_ref, bf_ref, *rest, tw, nh, tb, n_alias):
    (q_ref, kT_ref, vT_ref, kTb_ref, vTb_ref, lfT_ref, d_ref, dTb_ref, jlo_ref, qm_ref, sg_ref,
     carry_ref, tri_ref, hk_ref, hd_ref) = rest[n_alias:]
    t = pl.program_id(1)
    tm = x_ref.shape[1]
    hp = wfT_ref.shape[0]
    nsub = tm // tb

    @pl.when(t == 0)
    def _():
        carry_ref[...] = jnp.zeros(carry_ref.shape, F32)
        hk_ref[...] = jnp.zeros(hk_ref.shape, F32)
        hd_ref[...] = jnp.zeros(hd_ref.shape, F32)
        r = lax.broadcasted_iota(jnp.int32, (tm, tm), 0)
        c = lax.broadcasted_iota(jnp.int32, (tm, tm), 1)
        tri_ref[...] = jnp.where(r <= c, 1.0, 0.0).astype(BF16)

    xb = x_ref[0].astype(BF16)
    q2 = (_dot(xb, wn_ref[:, 0:tw]) * (QK_SCALE * LOG2E)).astype(BF16)
    q_ref[0] = q2
    qm_ref[0] = (_dot(xb, wn_ref[:, tw:tw + X_WIDTH]) * QK_SCALE).astype(BF16)
    sg_ref[0] = _silu(_dot(xb, wn_ref[:, tw + X_WIDTH:])).astype(BF16)

    kT = _dot_nt(wkvT_ref[0:tw, :], xb)
    vT = _dot_nt(wkvT_ref[tw:2 * tw, :], xb)
    kT_ref[0] = kT
    vT_ref[0] = vT
    kTb = kT.astype(BF16)
    vTb = vT.astype(BF16)
    for u in range(nsub):
        for h in range(nh):
            kTb_ref[0, u, h] = kTb[h * HEAD_DIM:(h + 1) * HEAD_DIM, u * tb:(u + 1) * tb]
            vTb_ref[0, u, h] = vTb[h * HEAD_DIM:(h + 1) * HEAD_DIM, u * tb:(u + 1) * tb]

    fT = _dot_nt(wfT_ref[...], xb) + bf_ref[...]
    hrow = lax.broadcasted_iota(jnp.int32, (hp, tm), 0)
    lf = jnp.where(hrow < nh, _log_sigmoid(fT), 0.0)
    lfT_ref[0] = lf

    cs3 = _dot(jnp.concatenate(_split3(lf), axis=0), tri_ref[...])
    cs = cs3[0:hp] + cs3[hp:2 * hp] + cs3[2 * hp:3 * hp] + carry_ref[:, 0:1]
    carry_ref[...] = jnp.broadcast_to(cs[:, tm - 1:tm], carry_ref.shape)
    cs2 = cs * LOG2E
    for u in range(nsub):
        for h in range(hp):
            dTb_ref[0, u, h] = cs2[h:h + 1, u * tb:(u + 1) * tb]
    d_pad = jnp.concatenate([cs2, jnp.zeros((LANES - hp, tm), F32)], axis=0)
    d_ref[0] = d_pad.T

    head_of_col = lax.broadcasted_iota(jnp.int32, (LANES, tw), 1) // HEAD_DIM
    sel = jnp.where(head_of_col == lax.broadcasted_iota(jnp.int32, (LANES, tw), 0), 1.0, 0.0).astype(BF16)
    q2f = q2.astype(F32)
    nq = _dot_nt(sel, (q2f * q2f).astype(BF16))
    kf = kTb.astype(F32)
    ksq = kf * kf
    nk = jnp.concatenate([jnp.sum(ksq[h * HEAD_DIM:(h + 1) * HEAD_DIM], axis=0, keepdims=True)
                          for h in range(nh)] + [jnp.zeros((hp - nh, tm), F32)], axis=0)
    lane = lax.broadcasted_iota(jnp.int32, (hp, LANES), 1)
    lane_f = lane.astype(F32)
    for u in range(nsub):
        i = t * nsub + u
        qmax = jnp.sqrt(jnp.max(nq[0:hp, u * tb:(u + 1) * tb], axis=1, keepdims=True))
        kmax = jnp.sqrt(jnp.max(nk[:, u * tb:(u + 1) * tb], axis=1, keepdims=True))
        hk = jnp.where(lane == i, kmax, hk_ref[...])
        hd = jnp.where(lane == i, cs2[:, (u + 1) * tb - 1:(u + 1) * tb], hd_ref[...])
        hk_ref[...] = hk
        hd_ref[...] = hd
        gap = qmax * (hk + kmax) * NORM_SLACK + cs2[:, u * tb:u * tb + 1] - hd
        alive = (gap >= -SKIP_LOG2) & (lane < i)
        i_f = i.astype(F32)
        jlo_ref[0, u] = jnp.min(jnp.where(alive, lane_f, i_f), axis=1, keepdims=True).astype(jnp.int32)


def _fox_proj_prompt(x, wn, wkvT, wfT, bf, kv_prev, *, layer, n_layers, tm, tb, nh):
    bsz, seq, d_model = x.shape
    tw = nh * HEAD_DIM
    hp = wfT.shape[0]
    nblk = seq // tb
    nsub = tm // tb
    assert nblk <= LANES
    aliased = [] if kv_prev is None else list(kv_prev)
    kern = functools.partial(_fox_proj_kernel, tw=tw, nh=nh, tb=tb, n_alias=len(aliased))
    row = lambda bb, t: (bb, t, 0)
    colT = lambda bb, t: (bb, 0, t)
    blocked = lambda bb, t: (bb, t, 0, 0, 0)
    kv_all = jax.ShapeDtypeStruct((n_layers, bsz, tw, seq), F32)
    kv_spec = pl.BlockSpec((None, 1, tw, tm), lambda bb, t: (layer, bb, 0, t))
    out_shape = [
        jax.ShapeDtypeStruct((bsz, seq, tw), BF16),
        kv_all,
        kv_all,
        jax.ShapeDtypeStruct((bsz, nblk, nh, HEAD_DIM, tb), BF16),
        jax.ShapeDtypeStruct((bsz, nblk, nh, HEAD_DIM, tb), BF16),
        jax.ShapeDtypeStruct((bsz, hp, seq), F32),
        jax.ShapeDtypeStruct((bsz, seq, LANES), F32),
        jax.ShapeDtypeStruct((bsz, nblk, hp, 1, tb), F32),
        jax.ShapeDtypeStruct((bsz, nblk, hp, 1), jnp.int32),
        jax.ShapeDtypeStruct((bsz, seq, X_WIDTH), BF16),
        jax.ShapeDtypeStruct((bsz, seq, d_model), BF16),
    ]
    out_specs = [
        pl.BlockSpec((1, tm, tw), row),
        kv_spec,
        kv_spec,
        pl.BlockSpec((1, nsub, nh, HEAD_DIM, tb), blocked),
        pl.BlockSpec((1, nsub, nh, HEAD_DIM, tb), blocked),
        pl.BlockSpec((1, hp, tm), colT),
        pl.BlockSpec((1, tm, LANES), row),
        pl.BlockSpec((1, nsub, hp, 1, tb), blocked),
        pl.BlockSpec((1, nsub, hp, 1), lambda bb, t: (bb, t, 0, 0)),
        pl.BlockSpec((1, tm, X_WIDTH), row),
        pl.BlockSpec((1, tm, d_model), row),
    ]
    n_in = 5
    return pl.pallas_call(
        kern,
        grid=(bsz, seq // tm),
        in_specs=[pl.BlockSpec((1, tm, d_model), row),
                  _const_spec(wn.shape), _const_spec(wkvT.shape), _const_spec(wfT.shape),
                  _const_spec(bf.shape)] + [pl.BlockSpec(memory_space=pl.ANY)] * len(aliased),
        out_specs=out_specs,
        out_shape=out_shape,
        input_output_aliases={n_in + k: 1 + k for k in range(len(aliased))},
        scratch_shapes=[pltpu.VMEM((hp, LANES), F32), pltpu.VMEM((tm, tm), BF16),
                        pltpu.VMEM((hp, LANES), F32), pltpu.VMEM((hp, LANES), F32)],
        compiler_params=pltpu.CompilerParams(dimension_semantics=("arbitrary", "arbitrary"),
                                             vmem_limit_bytes=VMEM_LIMIT),
        name="fox_proj_prompt",
    )(x, wn, wkvT, wfT, bf, *aliased)


def _fox_attn_kernel(jlo_ref, q_ref, kTb_ref, vTb_ref, d_ref, dTb_ref, x_ref, qm_ref, sg_ref, mkT_ref,
                     mvT_ref, wout_ref, g_ref, b_ref, xo_ref,
                     qh_scr, dq_scr, m_scr, l_scr, acc_scr, tok_scr, *, alpha, nh, hp, group):
    bb = pl.program_id(0)
    i = pl.program_id(1)
    nblk = pl.num_programs(1)
    tq = q_ref.shape[1]
    tk = kTb_ref.shape[4]
    reps = tk // LANES

    for h in range(nh):
        qh_scr[h] = q_ref[0, :, h * HEAD_DIM:(h + 1) * HEAD_DIM]
        dq_scr[h] = jnp.broadcast_to(d_ref[0, :, h:h + 1], (tq, LANES))

    row = lax.broadcasted_iota(jnp.int32, (tq, tk), 0)
    col = lax.broadcasted_iota(jnp.int32, (tq, tk), 1)
    causal = col <= row

    def group_body(g, carry):
        heads = [g * group + k for k in range(group)]
        m_scr[...] = jnp.full(m_scr.shape, -jnp.inf, F32)
        l_scr[...] = jnp.zeros(l_scr.shape, F32)
        acc_scr[...] = jnp.zeros(acc_scr.shape, F32)
        qs = [qh_scr[h] for h in heads]
        dqts = [jnp.concatenate([dq_scr[h]] * reps, axis=1) for h in heads]

        def step(j, masked):
            for k, h in enumerate(heads):
                s = _dot(qs[k], kTb_ref[0, j, h])
                s = s + dqts[k] - dTb_ref[0, j, h]
                if masked:
                    s = jnp.where(causal, s, -jnp.inf)
                m_old = m_scr[k]
                m_new = jnp.maximum(m_old, jnp.max(s, axis=1, keepdims=True))
                a = jnp.exp2(m_old - m_new)
                p = jnp.exp2(s - jnp.concatenate([m_new] * reps, axis=1))
                l_scr[k] = a * l_scr[k] + jnp.sum(p, axis=1, keepdims=True)
                m_scr[k] = m_new
                pv = _dot_nt(p.astype(BF16), vTb_ref[0, j, h])
                acc_scr[k] = a[:, 0:HEAD_DIM] * acc_scr[k] + pv

        def loop_body(j, c):
            step(j, False)
            return c

        base = (bb * nblk + i) * hp
        j0 = jlo_ref[base + heads[0]]
        for h in heads[1:]:
            j0 = jnp.minimum(j0, jlo_ref[base + h])
        lax.fori_loop(j0, i, loop_body, 0)
        step(i, True)
        for k, h in enumerate(heads):
            tok_scr[h] = acc_scr[k] / l_scr[k][:, 0:HEAD_DIM]
        return carry

    lax.fori_loop(0, nh // group, group_body, 0)

    tok = jnp.concatenate([tok_scr[h] for h in range(nh)], axis=1)
    crs = _cross_prompt(qm_ref[0].astype(F32), mkT_ref[0].astype(BF16), mvT_ref[0].astype(BF16))
    xo_ref[0] = _finish_prompt(x_ref[0], tok, crs, sg_ref[0].astype(F32), wout_ref,
                               g_ref[...], b_ref[...], alpha)


def _fox_attn_prompt(jlo, q, kTb, vTb, d, dTb, x, qm, sg, mkT, mvT, wout, g, b, *, alpha):
    bsz, seq, d_model = x.shape
    _, nblk, nh, _, tk = kTb.shape
    tq = tk
    hp = dTb.shape[2]
    n_mem = mkT.shape[2]
    group = max(k for k in range(1, MAX_HEAD_GROUP + 1) if nh % k == 0)
    kern = functools.partial(_fox_attn_kernel, alpha=alpha, nh=nh, hp=hp, group=group)
    row = lambda bb, t, jl: (bb, t, 0)
    per_b3 = lambda bb, t, jl: (bb, 0, 0)
    per_b5 = lambda bb, t, jl: (bb, 0, 0, 0, 0)

    def const(a):
        nd = a.ndim
        return pl.BlockSpec(a.shape, lambda bb, t, jl: (0,) * nd, pipeline_mode=pl.Buffered(1))

    grid_spec = pltpu.PrefetchScalarGridSpec(
        num_scalar_prefetch=1,
        grid=(bsz, nblk),
        in_specs=[pl.BlockSpec((1, tq, nh * HEAD_DIM), row),
                  pl.BlockSpec((1, nblk, nh, HEAD_DIM, tk), per_b5, pipeline_mode=pl.Buffered(1)),
                  pl.BlockSpec((1, nblk, nh, HEAD_DIM, tk), per_b5, pipeline_mode=pl.Buffered(1)),
                  pl.BlockSpec((1, tq, LANES), row),
                  pl.BlockSpec((1, nblk, hp, 1, tk), per_b5, pipeline_mode=pl.Buffered(1)),
                  pl.BlockSpec((1, tq, d_model), row),
                  pl.BlockSpec((1, tq, X_WIDTH), row),
                  pl.BlockSpec((1, tq, d_model), row),
                  pl.BlockSpec((1, X_WIDTH, n_mem), per_b3),
                  pl.BlockSpec((1, X_WIDTH, n_mem), per_b3),
                  const(wout), const(g), const(b)],
        out_specs=pl.BlockSpec((1, tq, d_model), row),
        scratch_shapes=[pltpu.VMEM((nh, tq, HEAD_DIM), BF16),
                        pltpu.VMEM((nh, tq, LANES), F32),
                        pltpu.VMEM((group, tq, LANES), F32),
                        pltpu.VMEM((group, tq, LANES), F32),
                        pltpu.VMEM((group, tq, HEAD_DIM), F32),
                        pltpu.VMEM((nh, tq, HEAD_DIM), F32)],
    )
    return pl.pallas_call(
        kern,
        grid_spec=grid_spec,
        out_shape=jax.ShapeDtypeStruct(x.shape, F32),
        compiler_params=pltpu.CompilerParams(dimension_semantics=("arbitrary", "arbitrary"),
                                             vmem_limit_bytes=VMEM_LIMIT),
        name="fox_attn_prompt",
    )(jlo.reshape(-1), q, kTb, vTb, d, dTb, x, qm, sg, mkT, mvT, wout, g, b)


def _s_pre_conv_kernel(xT_ref, wT_ref, st0_ref, st1_ref, cw_ref,
                       tokT_ref, uT_ref, qmT_ref, sgT_ref, *, tw):
    xb = xT_ref[...].astype(BF16)
    bg = _dot(wT_ref[0:tw, :], xb)
    u = _dot(wT_ref[tw:2 * tw, :], xb) * _dot(wT_ref[2 * tw:3 * tw, :], xb)
    cw = cw_ref[...]
    y = cw[:, 0:1] * st0_ref[...] + cw[:, 1:2] * st1_ref[...] + cw[:, 2:3] * u
    tokT_ref[...] = bg * y
    uT_ref[...] = u
    qmT_ref[...] = _dot(wT_ref[3 * tw:3 * tw + X_WIDTH, :], xb) * QK_SCALE
    sgT_ref[...] = _silu(_dot(wT_ref[3 * tw + X_WIDTH:, :], xb))


def _s_pre_conv(xT, wT, st0T, st1T, cwT):
    d_model, nb = xT.shape
    tw = cwT.shape[0]
    f = lambda r: jax.ShapeDtypeStruct((r, nb), F32)
    return pl.pallas_call(
        functools.partial(_s_pre_conv_kernel, tw=tw),
        out_shape=[f(tw), f(tw), f(X_WIDTH), f(d_model)],
        compiler_params=pltpu.CompilerParams(vmem_limit_bytes=VMEM_LIMIT),
        name="sample_pre_conv",
    )(xT, wT, st0T, st1T, cwT)


def _s_pre_fox_kernel(xT_ref, wT_ref, bf_ref,
                      qT_ref, kT_ref, vT_ref, lfT_ref, slT_ref, qmT_ref, sgT_ref, *, tw, nh):
    xb = xT_ref[...].astype(BF16)
    nb = xb.shape[1]
    hp = bf_ref.shape[0]
    q = _dot(wT_ref[0:tw, :], xb) * QK_SCALE
    k = _dot(wT_ref[tw:2 * tw, :], xb)
    qT_ref[...] = q
    kT_ref[...] = k
    vT_ref[...] = _dot(wT_ref[2 * tw:3 * tw, :], xb)
    qmT_ref[...] = _dot(wT_ref[3 * tw:3 * tw + X_WIDTH, :], xb) * QK_SCALE
    o = 3 * tw + X_WIDTH
    d_model = xb.shape[0]
    sgT_ref[...] = _silu(_dot(wT_ref[o:o + d_model, :], xb))
    fT = _dot(wT_ref[o + d_model:, :], xb) + bf_ref[...]
    hrow = lax.broadcasted_iota(jnp.int32, (hp, nb), 0)
    lfT_ref[...] = jnp.where(hrow < nh, _log_sigmoid(fT), 0.0)
    qk = q * k
    rows = [jnp.sum(qk[h * HEAD_DIM:(h + 1) * HEAD_DIM], axis=0, keepdims=True) for h in range(nh)]
    rows.append(jnp.zeros((hp - nh, nb), F32))
    slT_ref[...] = jnp.concatenate(rows, axis=0)


def _s_pre_fox(xT, wT, bf, *, nh):
    d_model, nb = xT.shape
    tw = nh * HEAD_DIM
    hp = bf.shape[0]
    f = lambda r: jax.ShapeDtypeStruct((r, nb), F32)
    return pl.pallas_call(
        functools.partial(_s_pre_fox_kernel, tw=tw, nh=nh),
        out_shape=[f(tw), f(tw), f(tw), f(hp), f(hp), f(X_WIDTH), f(d_model)],
        compiler_params=pltpu.CompilerParams(vmem_limit_bytes=VMEM_LIMIT),
        name="sample_pre_fox",
    )(xT, wT, bf)


def _column(ref_or_val, onehot):
    return jnp.sum(jnp.where(onehot, ref_or_val, 0.0), axis=1, keepdims=True)


def _cross_sample_cols(qcs, mks, mvs):
    pairs = [(i, h) for i in range(len(qcs)) for h in range(X_HEADS)]
    hd = lambda h: slice(h * HEAD_DIM, (h + 1) * HEAD_DIM)
    s = [jnp.sum(qcs[i][hd(h)] * mks[i][h], axis=0, keepdims=True) for i, h in pairs]
    m = [jnp.max(x, axis=1, keepdims=True) for x in s]
    e = [jnp.exp(x - y) for x, y in zip(s, m)]
    r = [1.0 / jnp.sum(x, axis=1, keepdims=True) for x in e]
    p = [x * y for x, y in zip(e, r)]
    o = [jnp.sum(x * mvs[i][h], axis=1, keepdims=True) for x, (i, h) in zip(p, pairs)]
    return [jnp.concatenate(o[i * X_HEADS:(i + 1) * X_HEADS], axis=0) for i in range(len(qcs))]


def _s_cross_kernel(qmT_ref, mk_ref, mv_ref, crsT_ref, *, bc):
    c = pl.program_id(0)

    @pl.when(c == 0)
    def _():
        crsT_ref[...] = jnp.zeros(crsT_ref.shape, F32)

    lane = lax.broadcasted_iota(jnp.int32, crsT_ref.shape, 1)
    qmT = qmT_ref[...]
    hots = [lane == c * bc + bb for bb in range(bc)]
    cols = _cross_sample_cols([_column(qmT, hot) for hot in hots],
                              [mk_ref.at[bb] for bb in range(bc)], [mv_ref.at[bb] for bb in range(bc)])
    out = crsT_ref[...]
    for bb in range(bc):
        out = jnp.where(hots[bb], cols[bb], out)
    crsT_ref[...] = out


def _s_cross(qmT, mk5, mv5, layer, *, bc):
    nb = qmT.shape[1]
    n_mem = mk5.shape[4]
    blk = pl.BlockSpec((None, bc, X_HEADS, HEAD_DIM, n_mem), lambda c: (layer, c, 0, 0, 0))
    return pl.pallas_call(
        functools.partial(_s_cross_kernel, bc=bc),
        grid=(nb // bc,),
        in_specs=[pl.BlockSpec(qmT.shape, lambda c: (0, 0)), blk, blk],
        out_specs=pl.BlockSpec(qmT.shape, lambda c: (0, 0)),
        out_shape=jax.ShapeDtypeStruct(qmT.shape, F32),
        compiler_params=pltpu.CompilerParams(dimension_semantics=("arbitrary",),
                                             vmem_limit_bytes=VMEM_LIMIT),
        name="sample_cross",
    )(qmT, mk5, mv5)


def _s_fox_scores_kernel(pt_ref, qT_ref, slT_ref, lfnT_ref, *rest, nh, n_pages):
    lf_pages = rest[0:n_pages]
    k_pages = rest[n_pages:2 * n_pages]
    p_ref, pself_ref, l_ref, alive_ref, l_scr, s_scr = rest[2 * n_pages:]
    b = pl.program_id(0)
    page = k_pages[0].shape[2]
    nrow = nh * n_pages

    def onehot(shape):
        return lax.broadcasted_iota(jnp.int32, shape, 1) == b

    qc = _column(qT_ref[...], onehot(qT_ref.shape))
    slc = _column(slT_ref[...], onehot(slT_ref.shape))
    lfn = _column(lfnT_ref[...], onehot(lfnT_ref.shape))

    for p in range(n_pages):
        for h in range(nh):
            l_scr[h * n_pages + p:h * n_pages + p + 1, :] = lf_pages[p][h]
    lfp = l_scr[...]
    r = lax.broadcasted_iota(jnp.int32, (page, page), 0)
    c = lax.broadcasted_iota(jnp.int32, (page, page), 1)
    after_in_page = jnp.where(r > c, 1.0, 0.0).astype(BF16)
    w3 = _dot(jnp.concatenate(_split3(lfp), axis=0), after_in_page)
    within = w3[0:nrow] + w3[nrow:2 * nrow] + w3[2 * nrow:3 * nrow]
    tot = jnp.broadcast_to(jnp.sum(lfp, axis=1, keepdims=True), (nrow, page))
    ru = lax.broadcasted_iota(jnp.int32, (nrow, nrow), 0)
    cu = lax.broadcasted_iota(jnp.int32, (nrow, nrow), 1)
    later_page = jnp.where((cu > ru) & (cu // n_pages == ru // n_pages), 1.0, 0.0).astype(BF16)
    a3 = _dot(later_page, jnp.concatenate(_split3(tot), axis=1))
    bias = within + a3[:, 0:page] + a3[:, page:2 * page] + a3[:, 2 * page:3 * page]

    for h in range(nh):
        qh = qc[h * HEAD_DIM:(h + 1) * HEAD_DIM]
        for p in range(n_pages):
            s_scr[h * n_pages + p:h * n_pages + p + 1, :] = jnp.sum(qh * k_pages[p][h], axis=0, keepdims=True)
    s_all = s_scr[...] + bias

    page_max = None
    for h in range(nh):
        rows = slice(h * n_pages, (h + 1) * n_pages)
        s_h = s_all[rows] + lfn[h:h + 1]
        sl = slc[h:h + 1]
        m = jnp.maximum(jnp.max(jnp.max(s_h, axis=1, keepdims=True), axis=0, keepdims=True), sl)
        p_h = jnp.exp(s_h - m)
        p_self = jnp.exp(sl - m)
        l = jnp.sum(jnp.sum(p_h, axis=1, keepdims=True), axis=0, keepdims=True) + p_self
        p_ref[0, rows, :] = p_h
        pself_ref[0, h:h + 1, :] = jnp.broadcast_to(p_self, (1, LANES))
        l_ref[0, h:h + 1, :] = jnp.broadcast_to(l, (1, LANES))
        pm = jnp.max(p_h, axis=1, keepdims=True)
        page_max = pm if page_max is None else jnp.maximum(page_max, pm)
    hp = pself_ref.shape[1]
    if hp > nh:
        pself_ref[0, nh:hp, :] = jnp.zeros((hp - nh, LANES), F32)
        l_ref[0, nh:hp, :] = jnp.ones((hp - nh, LANES), F32)
    alive_ref[0] = jnp.broadcast_to(page_max, (n_pages, LANES))


def _s_fox_values_kernel(pt_ref, alive_ref, p_ref, pself_ref, l_ref, vnT_ref, *rest, nh, n_pages):
    v_pages = rest[0:n_pages]
    tokT_ref, acc_scr = rest[n_pages:]
    b = pl.program_id(0)

    @pl.when(b == 0)
    def _():
        tokT_ref[...] = jnp.zeros(tokT_ref.shape, F32)

    def onehot(shape):
        return lax.broadcasted_iota(jnp.int32, shape, 1) == b

    vnc = _column(vnT_ref[...], onehot(vnT_ref.shape))
    acc_scr[...] = jnp.zeros(acc_scr.shape, F32)
    for p in range(n_pages):
        @pl.when(alive_ref[b * n_pages + p] > 0)
        def _(p=p):
            for h in range(nh):
                r = h * n_pages + p
                acc_scr[h] += p_ref[0, r:r + 1, :] * v_pages[p][h]

    oh_tok = onehot((HEAD_DIM, tokT_ref.shape[1]))
    for h in range(nh):
        p_self = pself_ref[0, h:h + 1, 0:1]
        o = jnp.sum(acc_scr[h], axis=1, keepdims=True) + p_self * vnc[h * HEAD_DIM:(h + 1) * HEAD_DIM]
        o = o / l_ref[0, h:h + 1, 0:1]
        sl_rows = slice(h * HEAD_DIM, (h + 1) * HEAD_DIM)
        tokT_ref[sl_rows, :] = jnp.where(oh_tok, o, tokT_ref[sl_rows, :])


def _s_fox(page_table, qT, vnT, slT, lfnT, lf5, k5, v5, fox_layer, *, nh):
    tw, nb = qT.shape
    n_pages = page_table.shape[1]
    page = k5.shape[4]
    hp = slT.shape[0]
    nrow = nh * n_pages
    cparams = pltpu.CompilerParams(dimension_semantics=("arbitrary",), vmem_limit_bytes=VMEM_LIMIT)

    full1 = lambda a: pl.BlockSpec(a.shape, lambda b, pt: (0, 0))
    per_b1 = lambda r: pl.BlockSpec((1, r, LANES), lambda b, pt: (b, 0, 0))
    lf_specs = [pl.BlockSpec((None, nh, None, 1, page), lambda b, pt, p=p: (fox_layer, 0, pt[b, p], 0, 0))
                for p in range(n_pages)]
    k_specs = [pl.BlockSpec((None, None, nh, HEAD_DIM, page), lambda b, pt, p=p: (fox_layer, pt[b, p], 0, 0, 0))
               for p in range(n_pages)]
    probs, p_self, l_sum, page_max = pl.pallas_call(
        functools.partial(_s_fox_scores_kernel, nh=nh, n_pages=n_pages),
        grid_spec=pltpu.PrefetchScalarGridSpec(
            num_scalar_prefetch=1,
            grid=(nb,),
            in_specs=[full1(qT), full1(slT), full1(lfnT)] + lf_specs + k_specs,
            out_specs=[pl.BlockSpec((1, nrow, page), lambda b, pt: (b, 0, 0)),
                       per_b1(hp), per_b1(hp), per_b1(n_pages)],
            scratch_shapes=[pltpu.VMEM((nrow, page), F32), pltpu.VMEM((nrow, page), F32)]),
        out_shape=[jax.ShapeDtypeStruct((nb, nrow, page), F32),
                   jax.ShapeDtypeStruct((nb, hp, LANES), F32),
                   jax.ShapeDtypeStruct((nb, hp, LANES), F32),
                   jax.ShapeDtypeStruct((nb, n_pages, LANES), F32)],
        compiler_params=cparams,
        name="sample_fox_scores",
    )(page_table, qT, slT, lfnT, *([lf5] * n_pages), *([k5] * n_pages))

    alive = page_max[:, :, 0] > 0.0
    last_live = lax.cummax(jnp.where(alive, jnp.arange(nb, dtype=jnp.int32)[:, None], -1), axis=0)
    pt_eff = jnp.take_along_axis(page_table, jnp.maximum(last_live, 0), axis=0)

    full2 = lambda a: pl.BlockSpec(a.shape, lambda b, pt, al: (0, 0))
    per_b2 = lambda r, c: pl.BlockSpec((1, r, c), lambda b, pt, al: (b, 0, 0))
    v_specs = [pl.BlockSpec((None, None, nh, HEAD_DIM, page),
                            lambda b, pt, al, p=p: (fox_layer, pt[b, p], 0, 0, 0)) for p in range(n_pages)]
    return pl.pallas_call(
        functools.partial(_s_fox_values_kernel, nh=nh, n_pages=n_pages),
        grid_spec=pltpu.PrefetchScalarGridSpec(
            num_scalar_prefetch=2,
            grid=(nb,),
            in_specs=[per_b2(nrow, page), per_b2(hp, LANES), per_b2(hp, LANES), full2(vnT)] + v_specs,
            out_specs=pl.BlockSpec((tw, nb), lambda b, pt, al: (0, 0)),
            scratch_shapes=[pltpu.VMEM((nh, HEAD_DIM, page), F32)]),
        out_shape=jax.ShapeDtypeStruct((tw, nb), F32),
        compiler_params=cparams,
        name="sample_fox_values",
    )(pt_eff, alive.astype(jnp.int32).reshape(-1), probs, p_self, l_sum, vnT, *([v5] * n_pages))


def _s_post_kernel(xT_ref, tokT_ref, crsT_ref, sgT_ref, woT_ref, g_ref, b_ref, out_ref, *, alpha):
    tw = tokT_ref.shape[0]
    sg = sgT_ref[...]
    mt = (tokT_ref[...] * sg[0:tw]).astype(BF16)
    mc = (crsT_ref[...] * sg[tw:]).astype(BF16)
    y = alpha * xT_ref[...] + _dot(woT_ref[:, 0:tw], mt) + _dot(woT_ref[:, tw:], mc)
    out_ref[...] = _layer_norm(y, g_ref[...], b_ref[...], axis=0)


def _s_post(xT, tokT, crsT, sgT, woT, gcol, bcol, *, alpha):
    return pl.pallas_call(
        functools.partial(_s_post_kernel, alpha=alpha),
        out_shape=jax.ShapeDtypeStruct(xT.shape, F32),
        compiler_params=pltpu.CompilerParams(vmem_limit_bytes=VMEM_LIMIT),
        name="sample_post",
    )(xT, tokT, crsT, sgT, woT, gcol, bcol)


def kernel(x_prompt, x_sample, mem_prompt, cache_fox_k, cache_fox_v, cache_fox_logf, state_conv,
           cache_mem_k, cache_mem_v, page_table, w_in_conv, conv_w, w_in_fox, b_forget, w_mem_kv,
           w_out, ln_g, ln_b):
    depth = w_out.shape[0]
    alpha = float((2 * depth) ** 0.25)
    bsz, seq, d_model = x_prompt.shape
    nb = x_sample.shape[0]
    tw = conv_w.shape[2]
    nh = tw // HEAD_DIM
    hp = -(-nh // 8) * 8
    tm = min(512, seq)
    tb = min(256, seq)
    bc = min(8, nb)
    kv_all = None

    mkT_all, mvT_all = _mem_kv(mem_prompt, jnp.swapaxes(w_mem_kv, 1, 2).astype(BF16))
    mk5 = jnp.transpose(cache_mem_k, (0, 1, 3, 4, 2))
    mv5 = jnp.transpose(cache_mem_v, (0, 1, 3, 4, 2))
    k5 = jnp.transpose(cache_fox_k, (0, 1, 3, 4, 2))
    v5 = jnp.transpose(cache_fox_v, (0, 1, 3, 4, 2))
    lf5 = jnp.transpose(cache_fox_logf, (0, 3, 1, 2))[:, :, :, None, :]

    xp = x_prompt
    xsT = x_sample.reshape(nb, d_model).T
    fl_p, fk_s, fv_s, fl_s, cv_p, cv_s = [], [], [], [], [], []

    for i in range(depth):
        j = i // 2
        woutb = w_out[i].astype(BF16)
        woutT = w_out[i].T.astype(BF16)
        g_row, b_row = ln_g[i][None, :], ln_b[i][None, :]
        g_col, b_col = ln_g[i][:, None], ln_b[i][:, None]
        if i % 2 == 0:
            w = w_in_conv[j]
            xp, st_p = _conv_layer_prompt(xp, w.astype(BF16), conv_w[j], mkT_all[i], mvT_all[i],
                                          woutb, g_row, b_row, alpha=alpha, tm=tm)
            cv_p.append(st_p)
            st = state_conv[j]
            tokT, uT, qmT, sgT = _s_pre_conv(xsT, w.T.astype(BF16), st[:, 0].T, st[:, 1].T, conv_w[j].T)
            cv_s.append(jnp.stack([st[:, 1], uT.T], axis=1))
            crsT = _s_cross(qmT, mk5, mv5, i, bc=bc)
        else:
            w = w_in_fox[j]
            o_f, o_qm, o_g = 3 * tw, 3 * tw + nh, 3 * tw + nh + X_WIDTH
            wq, wk, wv = w[:, 0:tw], w[:, tw:2 * tw], w[:, 2 * tw:3 * tw]
            wf, wqm, wg = w[:, o_f:o_qm], w[:, o_qm:o_g], w[:, o_g:]
            wfT = jnp.pad(wf.T, ((0, hp - nh), (0, 0)))
            bf = jnp.pad(b_forget[j], (0, hp - nh))[:, None]
            wn = jnp.concatenate([wq, wqm, wg], axis=1).astype(BF16)
            wkvT = jnp.concatenate([wk.T, wv.T], axis=0).astype(BF16)
            q, kT_all, vT_all, kTb, vTb, lfT, d, dTb, jlo, qm, sg = _fox_proj_prompt(
                xp, wn, wkvT, wfT.astype(BF16), bf, kv_all, layer=j, n_layers=depth // 2,
                tm=tm, tb=tb, nh=nh)
            kv_all = (kT_all, vT_all)
            xp = _fox_attn_prompt(jlo, q, kTb, vTb, d, dTb, xp, qm, sg, mkT_all[i], mvT_all[i],
                                  woutb, g_row, b_row, alpha=alpha)
            fl_p.append(jnp.transpose(lfT[:, :nh, :], (0, 2, 1)))

            wT = jnp.concatenate([wq.T, wk.T, wv.T, wqm.T, wg.T, wfT], axis=0).astype(BF16)
            qT, knT, vnT, lfnT, slT, qmT, sgT = _s_pre_fox(xsT, wT, bf, nh=nh)
            tokT = _s_fox(page_table, qT, vnT, slT, lfnT, lf5, k5, v5, j, nh=nh)
            crsT = _s_cross(qmT, mk5, mv5, i, bc=bc)
            fk_s.append(knT.T.reshape(nb, 1, nh, HEAD_DIM))
            fv_s.append(vnT.T.reshape(nb, 1, nh, HEAD_DIM))
            fl_s.append(lfnT[:nh].T.reshape(nb, 1, nh))
        xsT = _s_post(xsT, tokT, crsT, sgT, woutT, g_col, b_col, alpha=alpha)

    n_mem = mem_prompt.shape[1]
    mk_out = jnp.transpose(mkT_all.reshape(depth, bsz, X_HEADS, HEAD_DIM, n_mem), (0, 1, 4, 2, 3))
    mv_out = jnp.transpose(mvT_all.reshape(depth, bsz, X_HEADS, HEAD_DIM, n_mem), (0, 1, 4, 2, 3))
    n_fox = depth // 2
    fk_all = jnp.transpose(kv_all[0].reshape(n_fox, bsz, nh, HEAD_DIM, seq), (0, 1, 4, 2, 3))
    fv_all = jnp.transpose(kv_all[1].reshape(n_fox, bsz, nh, HEAD_DIM, seq), (0, 1, 4, 2, 3))
    return (xp, xsT.T.reshape(nb, 1, d_model),
            fk_all, fv_all, jnp.stack(fl_p),
            jnp.stack(fk_s), jnp.stack(fv_s), jnp.stack(fl_s),
            jnp.stack(cv_p), jnp.stack(cv_s), mk_out, mv_out)
```

```python
import functools

import jax
import jax.numpy as jnp
from jax import lax
from jax.experimental import pallas as pl
from jax.experimental.pallas import tpu as pltpu

F32 = jnp.float32
BF16 = jnp.bfloat16

HEAD_DIM = 64
X_HEADS = 4
X_WIDTH = X_HEADS * HEAD_DIM
CONV_WIDTH = 3
LN_EPS = 1e-5
QK_SCALE = HEAD_DIM ** -0.5
LOG2E = 1.4426950408889634
SKIP_LOG2 = 150.0
NORM_SLACK = 1.02
MAX_HEAD_GROUP = 6
LANES = 128
VMEM_LIMIT = 56 * 1024 * 1024

_NT = (((1,), (1,)), ((), ()))


def _dot(a, b):
    return jnp.dot(a, b, preferred_element_type=F32)


def _dot_nt(a, b):
    return lax.dot_general(a, b, _NT, preferred_element_type=F32)


def _silu(g):
    return g * jax.nn.sigmoid(g)


def _log_sigmoid(x):
    return jnp.minimum(x, 0.0) - jnp.log1p(jnp.exp(-jnp.abs(x)))


def _layer_norm(y, g, b, axis):
    mu = jnp.mean(y, axis=axis, keepdims=True)
    yc = y - mu
    var = jnp.mean(yc * yc, axis=axis, keepdims=True)
    return yc * lax.rsqrt(var + LN_EPS) * g + b


def _split3(x):
    hi = x.astype(BF16)
    r1 = x - hi.astype(F32)
    mid = r1.astype(BF16)
    lo = (r1 - mid.astype(F32)).astype(BF16)
    return hi, mid, lo


def _const_spec(shape):
    nd = len(shape)
    return pl.BlockSpec(shape, lambda *_: (0,) * nd, pipeline_mode=pl.Buffered(1))


def _mem_kv_kernel(mem_ref, wT_ref, kT_ref, vT_ref):
    memb = mem_ref[0].astype(BF16)
    kvT = _dot_nt(wT_ref[0], memb)
    kT_ref[0, 0] = kvT[:X_WIDTH]
    vT_ref[0, 0] = kvT[X_WIDTH:]


def _mem_kv(mem_prompt, w_mem_kvT):
    depth = w_mem_kvT.shape[0]
    bsz, n_mem, d_model = mem_prompt.shape
    out = jax.ShapeDtypeStruct((depth, bsz, X_WIDTH, n_mem), F32)
    return pl.pallas_call(
        _mem_kv_kernel,
        grid=(depth, bsz),
        in_specs=[pl.BlockSpec((1, n_mem, d_model), lambda i, b: (b, 0, 0)),
                  pl.BlockSpec((1, 2 * X_WIDTH, d_model), lambda i, b: (i, 0, 0))],
        out_specs=[pl.BlockSpec((1, 1, X_WIDTH, n_mem), lambda i, b: (i, b, 0, 0))] * 2,
        out_shape=[out, out],
        compiler_params=pltpu.CompilerParams(dimension_semantics=("arbitrary", "arbitrary")),
        name="mem_kv",
    )(mem_prompt, w_mem_kvT)


def _cross_prompt(qm, kT, vT):
    lane = lax.broadcasted_iota(jnp.int32, qm.shape, 1)
    vrow = lax.broadcasted_iota(jnp.int32, vT.shape, 0)
    out = None
    for h in range(X_HEADS):
        lo, hi = h * HEAD_DIM, (h + 1) * HEAD_DIM
        qh = jnp.where((lane >= lo) & (lane < hi), qm, 0.0).astype(BF16)
        s = _dot(qh, kT)
        m = jnp.max(s, axis=-1, keepdims=True)
        e = jnp.exp(s - m)
        p = e * (1.0 / jnp.sum(e, axis=-1, keepdims=True))
        vh = jnp.where((vrow >= lo) & (vrow < hi), vT, jnp.zeros_like(vT))
        o = _dot_nt(p.astype(BF16), vh)
        out = o if out is None else out + o
    return out


def _finish_prompt(x, tok, crs, sg, wout_ref, g, b, alpha):
    tw = tok.shape[1]
    mt = (tok * sg[:, :tw]).astype(BF16)
    mc = (crs * sg[:, tw:]).astype(BF16)
    y = alpha * x + _dot(mt, wout_ref[0:tw, :]) + _dot(mc, wout_ref[tw:, :])
    return _layer_norm(y, g, b, axis=-1)


def _conv_layer_kernel(x_ref, win_ref, cw_ref, mkT_ref, mvT_ref, wout_ref, g_ref, b_ref,
                       xo_ref, st_ref, ubuf, *, alpha, tw):
    t = pl.program_id(1)
    tm = x_ref.shape[1]

    @pl.when(t == 0)
    def _():
        ubuf[0:8, :] = jnp.zeros((8, tw), F32)

    x = x_ref[0]
    xb = x.astype(BF16)
    bg = _dot(xb, win_ref[:, 0:tw])
    u = _dot(xb, win_ref[:, tw:2 * tw]) * _dot(xb, win_ref[:, 2 * tw:3 * tw])
    ubuf[8:8 + tm, :] = u
    cw = cw_ref[...]
    y = cw[0:1] * ubuf[6:6 + tm, :] + cw[1:2] * ubuf[7:7 + tm, :] + cw[2:3] * u
    tok = bg * y
    last = ubuf[tm + 6:tm + 8, :]
    ubuf[6:8, :] = last
    st_ref[0] = last

    qm = _dot(xb, win_ref[:, 3 * tw:3 * tw + X_WIDTH]) * QK_SCALE
    gate = _dot(xb, win_ref[:, 3 * tw + X_WIDTH:])
    crs = _cross_prompt(qm, mkT_ref[0].astype(BF16), mvT_ref[0].astype(BF16))
    xo_ref[0] = _finish_prompt(x, tok, crs, _silu(gate), wout_ref, g_ref[...], b_ref[...], alpha)


def _conv_layer_prompt(x, win, cw, mkT, mvT, wout, g, b, *, alpha, tm):
    bsz, seq, d_model = x.shape
    tw = cw.shape[1]
    n_mem = mkT.shape[2]
    kern = functools.partial(_conv_layer_kernel, alpha=alpha, tw=tw)
    return pl.pallas_call(
        kern,
        grid=(bsz, seq // tm),
        in_specs=[pl.BlockSpec((1, tm, d_model), lambda bb, t: (bb, t, 0)),
                  _const_spec(win.shape),
                  _const_spec(cw.shape),
                  pl.BlockSpec((1, X_WIDTH, n_mem), lambda bb, t: (bb, 0, 0)),
                  pl.BlockSpec((1, X_WIDTH, n_mem), lambda bb, t: (bb, 0, 0)),
                  _const_spec(wout.shape),
                  _const_spec(g.shape),
                  _const_spec(b.shape)],
        out_specs=[pl.BlockSpec((1, tm, d_model), lambda bb, t: (bb, t, 0)),
                   pl.BlockSpec((1, CONV_WIDTH - 1, tw), lambda bb, t: (bb, 0, 0))],
        out_shape=[jax.ShapeDtypeStruct(x.shape, F32),
                   jax.ShapeDtypeStruct((bsz, CONV_WIDTH - 1, tw), F32)],
        scratch_shapes=[pltpu.VMEM((tm + 8, tw), F32)],
        compiler_params=pltpu.CompilerParams(dimension_semantics=("arbitrary", "arbitrary"),
                                             vmem_limit_bytes=VMEM_LIMIT),
        name="conv_layer_prompt",
    )(x, win, cw, mkT, mvT, wout, g, b)


def _fox_proj_kernel(x_ref, wT_ref, bf_ref, *rest, tw, nh, tb, n_alias):
    (q_ref, kT_ref, vT_ref, kTb_ref, vTb_ref, lfT_ref, d_ref, dTb_ref, jlo_ref, qm_ref, sg_ref,
     carry_ref, tri_ref, hk_ref, hd_ref) = rest[n_alias:]
    t = pl.program_id(1)
    tm = x_ref.shape[1]
    d_model = x_ref.shape[2]
    hp = bf_ref.shape[0]
    o_qm, o_g, o_f = 3 * tw, 3 * tw + X_WIDTH, 3 * tw + X_WIDTH + d_model
    nsub = tm // tb

    @pl.when(t == 0)
    def _():
        carry_ref[...] = jnp.zeros(carry_ref.shape, F32)
        hk_ref[...] = jnp.zeros(hk_ref.shape, F32)
        hd_ref[...] = jnp.zeros(hd_ref.shape, F32)
        r = lax.broadcasted_iota(jnp.int32, (tm, tm), 0)
        c = lax.broadcasted_iota(jnp.int32, (tm, tm), 1)
        tri_ref[...] = jnp.where(r <= c, 1.0, 0.0).astype(BF16)

    xb = x_ref[0].astype(BF16)
    q2 = (_dot_nt(xb, wT_ref[0:tw, :]) * (QK_SCALE * LOG2E)).astype(BF16)
    q_ref[0] = q2
    qm_ref[0] = (_dot_nt(xb, wT_ref[o_qm:o_g, :]) * QK_SCALE).astype(BF16)
    sg_ref[0] = _silu(_dot_nt(xb, wT_ref[o_g:o_f, :])).astype(BF16)

    kT = _dot_nt(wT_ref[tw:2 * tw, :], xb)
    vT = _dot_nt(wT_ref[2 * tw:3 * tw, :], xb)
    kT_ref[0] = kT
    vT_ref[0] = vT
    kTb = kT.astype(BF16)
    vTb = vT.astype(BF16)
    for u in range(nsub):
        for h in range(nh):
            kTb_ref[0, u, h] = kTb[h * HEAD_DIM:(h + 1) * HEAD_DIM, u * tb:(u + 1) * tb]
            vTb_ref[0, u, h] = vTb[h * HEAD_DIM:(h + 1) * HEAD_DIM, u * tb:(u + 1) * tb]

    fT = _dot_nt(wT_ref[o_f:o_f + hp, :], xb) + bf_ref[...]
    hrow = lax.broadcasted_iota(jnp.int32, (hp, tm), 0)
    lf = jnp.where(hrow < nh, _log_sigmoid(fT), 0.0)
    lfT_ref[0] = lf

    cs3 = _dot(jnp.concatenate(_split3(lf), axis=0), tri_ref[...])
    cs = cs3[0:hp] + cs3[hp:2 * hp] + cs3[2 * hp:3 * hp] + carry_ref[:, 0:1]
    carry_ref[...] = jnp.broadcast_to(cs[:, tm - 1:tm], carry_ref.shape)
    cs2 = cs * LOG2E
    for u in range(nsub):
        for h in range(hp):
            dTb_ref[0, u, h] = cs2[h:h + 1, u * tb:(u + 1) * tb]
    d3 = [t.astype(F32) for t in _split3(cs2)] + [jnp.zeros((LANES - 3 * hp, tm), F32)]
    d_ref[0] = jnp.concatenate(d3, axis=0).T.astype(BF16)

    head_of_col = lax.broadcasted_iota(jnp.int32, (LANES, tw), 1) // HEAD_DIM
    sel = jnp.where(head_of_col == lax.broadcasted_iota(jnp.int32, (LANES, tw), 0), 1.0, 0.0).astype(BF16)
    q2f = q2.astype(F32)
    nq = _dot_nt(sel, (q2f * q2f).astype(BF16))
    kf = kTb.astype(F32)
    ksq = kf * kf
    nk = jnp.concatenate([jnp.sum(ksq[h * HEAD_DIM:(h + 1) * HEAD_DIM], axis=0, keepdims=True)
                          for h in range(nh)] + [jnp.zeros((hp - nh, tm), F32)], axis=0)
    lane = lax.broadcasted_iota(jnp.int32, (hp, LANES), 1)
    lane_f = lane.astype(F32)
    for u in range(nsub):
        i = t * nsub + u
        qmax = jnp.sqrt(jnp.max(nq[0:hp, u * tb:(u + 1) * tb], axis=1, keepdims=True))
        kmax = jnp.sqrt(jnp.max(nk[:, u * tb:(u + 1) * tb], axis=1, keepdims=True))
        hk = jnp.where(lane == i, kmax, hk_ref[...])
        hd = jnp.where(lane == i, cs2[:, (u + 1) * tb - 1:(u + 1) * tb], hd_ref[...])
        hk_ref[...] = hk
        hd_ref[...] = hd
        gap = qmax * (hk + kmax) * NORM_SLACK + cs2[:, u * tb:u * tb + 1] - hd
        alive = (gap >= -SKIP_LOG2) & (lane < i)
        i_f = i.astype(F32)
        jlo_ref[0, u] = jnp.min(jnp.where(alive, lane_f, i_f), axis=1, keepdims=True).astype(jnp.int32)


def _fox_proj_prompt(x, wT, bf, kv_prev, *, layer, n_layers, tm, tb, nh):
    bsz, seq, d_model = x.shape
    tw = nh * HEAD_DIM
    hp = bf.shape[0]
    nblk = seq // tb
    nsub = tm // tb
    assert nblk <= LANES
    aliased = [] if kv_prev is None else list(kv_prev)
    kern = functools.partial(_fox_proj_kernel, tw=tw, nh=nh, tb=tb, n_alias=len(aliased))
    row = lambda bb, t: (bb, t, 0)
    colT = lambda bb, t: (bb, 0, t)
    blocked = lambda bb, t: (bb, t, 0, 0, 0)
    kv_all = jax.ShapeDtypeStruct((n_layers, bsz, tw, seq), F32)
    kv_spec = pl.BlockSpec((None, 1, tw, tm), lambda bb, t: (layer, bb, 0, t))
    out_shape = [
        jax.ShapeDtypeStruct((bsz, seq, tw), BF16),
        kv_all,
        kv_all,
        jax.ShapeDtypeStruct((bsz, nblk, nh, HEAD_DIM, tb), BF16),
        jax.ShapeDtypeStruct((bsz, nblk, nh, HEAD_DIM, tb), BF16),
        jax.ShapeDtypeStruct((bsz, hp, seq), F32),
        jax.ShapeDtypeStruct((bsz, seq, LANES), BF16),
        jax.ShapeDtypeStruct((bsz, nblk, hp, 1, tb), F32),
        jax.ShapeDtypeStruct((bsz, nblk, hp, 1), jnp.int32),
        jax.ShapeDtypeStruct((bsz, seq, X_WIDTH), BF16),
        jax.ShapeDtypeStruct((bsz, seq, d_model), BF16),
    ]
    out_specs = [
        pl.BlockSpec((1, tm, tw), row),
        kv_spec,
        kv_spec,
        pl.BlockSpec((1, nsub, nh, HEAD_DIM, tb), blocked),
        pl.BlockSpec((1, nsub, nh, HEAD_DIM, tb), blocked),
        pl.BlockSpec((1, hp, tm), colT),
        pl.BlockSpec((1, tm, LANES), row),
        pl.BlockSpec((1, nsub, hp, 1, tb), blocked),
        pl.BlockSpec((1, nsub, hp, 1), lambda bb, t: (bb, t, 0, 0)),
        pl.BlockSpec((1, tm, X_WIDTH), row),
        pl.BlockSpec((1, tm, d_model), row),
    ]
    n_in = 3
    return pl.pallas_call(
        kern,
        grid=(bsz, seq // tm),
        in_specs=[pl.BlockSpec((1, tm, d_model), row), _const_spec(wT.shape),
                  _const_spec(bf.shape)] + [pl.BlockSpec(memory_space=pl.ANY)] * len(aliased),
        out_specs=out_specs,
        out_shape=out_shape,
        input_output_aliases={n_in + k: 1 + k for k in range(len(aliased))},
        scratch_shapes=[pltpu.VMEM((hp, LANES), F32), pltpu.VMEM((tm, tm), BF16),
                        pltpu.VMEM((hp, LANES), F32), pltpu.VMEM((hp, LANES), F32)],
        compiler_params=pltpu.CompilerParams(dimension_semantics=("arbitrary", "arbitrary"),
                                             vmem_limit_bytes=VMEM_LIMIT),
        name="fox_proj_prompt",
    )(x, wT, bf, *aliased)


def _fox_attn_kernel(jlo_ref, q_ref, kTb_ref, vTb_ref, d_ref, dTb_ref, x_ref, qm_ref, sg_ref, mkT_ref,
                     mvT_ref, wout_ref, g_ref, b_ref, xo_ref,
                     qh_scr, dq_scr, m_scr, l_scr, acc_scr, tok_scr, *, alpha, nh, hp, group):
    bb = pl.program_id(0)
    i = pl.program_id(1)
    nblk = pl.num_programs(1)
    tq = q_ref.shape[1]
    tk = kTb_ref.shape[4]
    reps = tk // LANES

    d3 = d_ref[0]
    term = lax.broadcasted_iota(jnp.int32, (LANES, LANES), 0)
    for h in range(nh):
        qh_scr[h] = q_ref[0, :, h * HEAD_DIM:(h + 1) * HEAD_DIM]
        pick = (term == h) | (term == hp + h) | (term == 2 * hp + h)
        dq_scr[h] = _dot(d3, jnp.where(pick, 1.0, 0.0).astype(BF16))

    row = lax.broadcasted_iota(jnp.int32, (tq, tk), 0)
    col = lax.broadcasted_iota(jnp.int32, (tq, tk), 1)
    causal = col <= row

    def group_body(g, carry):
        heads = [g * group + k for k in range(group)]
        m_scr[...] = jnp.full(m_scr.shape, -jnp.inf, F32)
        l_scr[...] = jnp.zeros(l_scr.shape, F32)
        acc_scr[...] = jnp.zeros(acc_scr.shape, F32)
        qs = [qh_scr[h] for h in heads]
        dqts = [jnp.concatenate([dq_scr[h]] * reps, axis=1) for h in heads]

        def step(j, masked):
            for k, h in enumerate(heads):
                s = _dot(qs[k], kTb_ref[0, j, h])
                s = s + dqts[k] - dTb_ref[0, j, h]
                if masked:
                    s = jnp.where(causal, s, -jnp.inf)
                m_old = m_scr[k]
                m_new = jnp.maximum(m_old, jnp.max(s, axis=1, keepdims=True))
                a = jnp.exp2(m_old - m_new)
                p = jnp.exp2(s - jnp.concatenate([m_new] * reps, axis=1))
                l_scr[k] = a * l_scr[k] + jnp.sum(p, axis=1, keepdims=True)
                m_scr[k] = m_new
                pv = _dot_nt(p.astype(BF16), vTb_ref[0, j, h])
                acc_scr[k] = a[:, 0:HEAD_DIM] * acc_scr[k] + pv

        def loop_body(j, c):
            step(j, False)
            return c

        base = (bb * nblk + i) * hp
        j0 = jlo_ref[base + heads[0]]
        for h in heads[1:]:
            j0 = jnp.minimum(j0, jlo_ref[base + h])
        lax.fori_loop(j0, i, loop_body, 0)
        step(i, True)
        for k, h in enumerate(heads):
            tok_scr[h] = acc_scr[k] / l_scr[k][:, 0:HEAD_DIM]
        return carry

    lax.fori_loop(0, nh // group, group_body, 0)

    tok = jnp.concatenate([tok_scr[h] for h in range(nh)], axis=1)
    crs = _cross_prompt(qm_ref[0].astype(F32), mkT_ref[0].astype(BF16), mvT_ref[0].astype(BF16))
    xo_ref[0] = _finish_prompt(x_ref[0], tok, crs, sg_ref[0].astype(F32), wout_ref,
                               g_ref[...], b_ref[...], alpha)


def _fox_attn_prompt(jlo, q, kTb, vTb, d, dTb, x, qm, sg, mkT, mvT, wout, g, b, *, alpha):
    bsz, seq, d_model = x.shape
    _, nblk, nh, _, tk = kTb.shape
    tq = tk
    hp = dTb.shape[2]
    n_mem = mkT.shape[2]
    group = max(k for k in range(1, MAX_HEAD_GROUP + 1) if nh % k == 0)
    kern = functools.partial(_fox_attn_kernel, alpha=alpha, nh=nh, hp=hp, group=group)
    row = lambda bb, t, jl: (bb, t, 0)
    per_b3 = lambda bb, t, jl: (bb, 0, 0)
    per_b5 = lambda bb, t, jl: (bb, 0, 0, 0, 0)

    def const(a):
        nd = a.ndim
        return pl.BlockSpec(a.shape, lambda bb, t, jl: (0,) * nd, pipeline_mode=pl.Buffered(1))

    grid_spec = pltpu.PrefetchScalarGridSpec(
        num_scalar_prefetch=1,
        grid=(bsz, nblk),
        in_specs=[pl.BlockSpec((1, tq, nh * HEAD_DIM), row),
                  pl.BlockSpec((1, nblk, nh, HEAD_DIM, tk), per_b5, pipeline_mode=pl.Buffered(1)),
                  pl.BlockSpec((1, nblk, nh, HEAD_DIM, tk), per_b5, pipeline_mode=pl.Buffered(1)),
                  pl.BlockSpec((1, tq, LANES), row),
                  pl.BlockSpec((1, nblk, hp, 1, tk), per_b5, pipeline_mode=pl.Buffered(1)),
                  pl.BlockSpec((1, tq, d_model), row),
                  pl.BlockSpec((1, tq, X_WIDTH), row),
                  pl.BlockSpec((1, tq, d_model), row),
                  pl.BlockSpec((1, X_WIDTH, n_mem), per_b3),
                  pl.BlockSpec((1, X_WIDTH, n_mem), per_b3),
                  const(wout), const(g), const(b)],
        out_specs=pl.BlockSpec((1, tq, d_model), row),
        scratch_shapes=[pltpu.VMEM((nh, tq, HEAD_DIM), BF16),
                        pltpu.VMEM((nh, tq, LANES), F32),
                        pltpu.VMEM((group, tq, LANES), F32),
                        pltpu.VMEM((group, tq, LANES), F32),
                        pltpu.VMEM((group, tq, HEAD_DIM), F32),
                        pltpu.VMEM((nh, tq, HEAD_DIM), F32)],
    )
    return pl.pallas_call(
        kern,
        grid_spec=grid_spec,
        out_shape=jax.ShapeDtypeStruct(x.shape, F32),
        compiler_params=pltpu.CompilerParams(dimension_semantics=("arbitrary", "arbitrary"),
                                             vmem_limit_bytes=VMEM_LIMIT),
        name="fox_attn_prompt",
    )(jlo.reshape(-1), q, kTb, vTb, d, dTb, x, qm, sg, mkT, mvT, wout, g, b)


def _s_pre_conv_kernel(xT_ref, w_ref, st0_ref, st1_ref, cw_ref,
                       tokT_ref, uT_ref, qmT_ref, sgT_ref, *, tw):
    xb = xT_ref[...].T.astype(BF16)
    proj = lambda lo, hi: _dot(xb, w_ref[:, lo:hi]).T
    bg = proj(0, tw)
    u = proj(tw, 2 * tw) * proj(2 * tw, 3 * tw)
    cw = cw_ref[...]
    y = cw[:, 0:1] * st0_ref[...] + cw[:, 1:2] * st1_ref[...] + cw[:, 2:3] * u
    tokT_ref[...] = bg * y
    uT_ref[...] = u
    qmT_ref[...] = proj(3 * tw, 3 * tw + X_WIDTH) * QK_SCALE
    sgT_ref[...] = _silu(proj(3 * tw + X_WIDTH, w_ref.shape[1]))


def _s_pre_conv(xT, wT, st0T, st1T, cwT):
    d_model, nb = xT.shape
    tw = cwT.shape[0]
    f = lambda r: jax.ShapeDtypeStruct((r, nb), F32)
    return pl.pallas_call(
        functools.partial(_s_pre_conv_kernel, tw=tw),
        out_shape=[f(tw), f(tw), f(X_WIDTH), f(d_model)],
        compiler_params=pltpu.CompilerParams(vmem_limit_bytes=VMEM_LIMIT),
        name="sample_pre_conv",
    )(xT, wT, st0T, st1T, cwT)


def _s_pre_fox_kernel(xT_ref, wT_ref, bf_ref,
                      qT_ref, kT_ref, vT_ref, lfT_ref, slT_ref, qmT_ref, sgT_ref, *, tw, nh):
    xb = xT_ref[...].astype(BF16)
    nb = xb.shape[1]
    hp = bf_ref.shape[0]
    q = _dot(wT_ref[0:tw, :], xb) * QK_SCALE
    k = _dot(wT_ref[tw:2 * tw, :], xb)
    qT_ref[...] = q
    kT_ref[...] = k
    vT_ref[...] = _dot(wT_ref[2 * tw:3 * tw, :], xb)
    qmT_ref[...] = _dot(wT_ref[3 * tw:3 * tw + X_WIDTH, :], xb) * QK_SCALE
    o = 3 * tw + X_WIDTH
    d_model = xb.shape[0]
    sgT_ref[...] = _silu(_dot(wT_ref[o:o + d_model, :], xb))
    fT = _dot(wT_ref[o + d_model:, :], xb) + bf_ref[...]
    hrow = lax.broadcasted_iota(jnp.int32, (hp, nb), 0)
    lfT_ref[...] = jnp.where(hrow < nh, _log_sigmoid(fT), 0.0)
    qk = q * k
    rows = [jnp.sum(qk[h * HEAD_DIM:(h + 1) * HEAD_DIM], axis=0, keepdims=True) for h in range(nh)]
    rows.append(jnp.zeros((hp - nh, nb), F32))
    slT_ref[...] = jnp.concatenate(rows, axis=0)


def _s_pre_fox(xT, wT, bf, *, nh):
    d_model, nb = xT.shape
    tw = nh * HEAD_DIM
    hp = bf.shape[0]
    f = lambda r: jax.ShapeDtypeStruct((r, nb), F32)
    return pl.pallas_call(
        functools.partial(_s_pre_fox_kernel, tw=tw, nh=nh),
        out_shape=[f(tw), f(tw), f(tw), f(hp), f(hp), f(X_WIDTH), f(d_model)],
        compiler_params=pltpu.CompilerParams(vmem_limit_bytes=VMEM_LIMIT),
        name="sample_pre_fox",
    )(xT, wT, bf)


def _column(ref_or_val, onehot):
    return jnp.sum(jnp.where(onehot, ref_or_val, 0.0), axis=1, keepdims=True)


def _cross_sample_cols(qcs, mks, mvs):
    pairs = [(i, h) for i in range(len(qcs)) for h in range(X_HEADS)]
    hd = lambda h: slice(h * HEAD_DIM, (h + 1) * HEAD_DIM)
    s = [jnp.sum(qcs[i][hd(h)] * mks[i][h], axis=0, keepdims=True) for i, h in pairs]
    m = [jnp.max(x, axis=1, keepdims=True) for x in s]
    e = [jnp.exp(x - y) for x, y in zip(s, m)]
    r = [1.0 / jnp.sum(x, axis=1, keepdims=True) for x in e]
    p = [x * y for x, y in zip(e, r)]
    o = [jnp.sum(x * mvs[i][h], axis=1, keepdims=True) for x, (i, h) in zip(p, pairs)]
    return [jnp.concatenate(o[i * X_HEADS:(i + 1) * X_HEADS], axis=0) for i in range(len(qcs))]


def _s_cross_kernel(qmT_ref, mk_ref, mv_ref, crsT_ref, *, bc):
    c = pl.program_id(0)

    @pl.when(c == 0)
    def _():
        crsT_ref[...] = jnp.zeros(crsT_ref.shape, F32)

    lane = lax.broadcasted_iota(jnp.int32, crsT_ref.shape, 1)
    qmT = qmT_ref[...]
    hots = [lane == c * bc + bb for bb in range(bc)]
    cols = _cross_sample_cols([_column(qmT, hot) for hot in hots],
                              [mk_ref.at[bb] for bb in range(bc)], [mv_ref.at[bb] for bb in range(bc)])
    out = crsT_ref[...]
    for bb in range(bc):
        out = jnp.where(hots[bb], cols[bb], out)
    crsT_ref[...] = out


def _s_cross(qmT, mk5, mv5, layer, *, bc):
    nb = qmT.shape[1]
    n_mem = mk5.shape[4]
    blk = pl.BlockSpec((None, bc, X_HEADS, HEAD_DIM, n_mem), lambda c: (layer, c, 0, 0, 0))
    return pl.pallas_call(
        functools.partial(_s_cross_kernel, bc=bc),
        grid=(nb // bc,),
        in_specs=[pl.BlockSpec(qmT.shape, lambda c: (0, 0)), blk, blk],
        out_specs=pl.BlockSpec(qmT.shape, lambda c: (0, 0)),
        out_shape=jax.ShapeDtypeStruct(qmT.shape, F32),
        compiler_params=pltpu.CompilerParams(dimension_semantics=("arbitrary",),
                                             vmem_limit_bytes=VMEM_LIMIT),
        name="sample_cross",
    )(qmT, mk5, mv5)


def _s_fox_scores_kernel(pt_ref, qT_ref, slT_ref, lfnT_ref, *rest, nh, n_pages, rows):
    n = rows * n_pages
    p_ref, pself_ref, l_ref, alive_ref, l_scr, s_scr = rest[2 * n:]
    for r in range(rows):
        _s_fox_scores_row(pl.program_id(0) * rows + r, qT_ref, slT_ref, lfnT_ref,
                          rest[r * n_pages:(r + 1) * n_pages], rest[n + r * n_pages:n + (r + 1) * n_pages],
                          p_ref.at[r], pself_ref.at[r], l_ref.at[r], alive_ref.at[r],
                          l_scr.at[r], s_scr.at[r], nh=nh, n_pages=n_pages)


def _s_fox_scores_row(b, qT_ref, slT_ref, lfnT_ref, lf_pages, k_pages,
                      p_ref, pself_ref, l_ref, alive_ref, l_scr, s_scr, *, nh, n_pages):
    page = k_pages[0].shape[2]
    nrow = nh * n_pages

    def onehot(shape):
        return lax.broadcasted_iota(jnp.int32, shape, 1) == b

    qc = _column(qT_ref[...], onehot(qT_ref.shape))
    slc = _column(slT_ref[...], onehot(slT_ref.shape))
    lfn = _column(lfnT_ref[...], onehot(lfnT_ref.shape))

    for p in range(n_pages):
        for h in range(nh):
            l_scr[h * n_pages + p:h * n_pages + p + 1, :] = lf_pages[p][h]
    lfp = l_scr[...]
    r = lax.broadcasted_iota(jnp.int32, (page, page), 0)
    c = lax.broadcasted_iota(jnp.int32, (page, page), 1)
    after_in_page = jnp.where(r > c, 1.0, 0.0).astype(BF16)
    w3 = _dot(jnp.concatenate(_split3(lfp), axis=0), after_in_page)
    within = w3[0:nrow] + w3[nrow:2 * nrow] + w3[2 * nrow:3 * nrow]
    tot = jnp.broadcast_to(jnp.sum(lfp, axis=1, keepdims=True), (nrow, page))
    ru = lax.broadcasted_iota(jnp.int32, (nrow, nrow), 0)
    cu = lax.broadcasted_iota(jnp.int32, (nrow, nrow), 1)
    later_page = jnp.where((cu > ru) & (cu // n_pages == ru // n_pages), 1.0, 0.0).astype(BF16)
    a3 = _dot(later_page, jnp.concatenate(_split3(tot), axis=1))
    bias = within + a3[:, 0:page] + a3[:, page:2 * page] + a3[:, 2 * page:3 * page]

    for h in range(nh):
        qh = qc[h * HEAD_DIM:(h + 1) * HEAD_DIM]
        for p in range(n_pages):
            s_scr[h * n_pages + p:h * n_pages + p + 1, :] = jnp.sum(qh * k_pages[p][h], axis=0, keepdims=True)
    s_all = s_scr[...] + bias

    hs = range(nh)
    s_h = [s_all[h * n_pages:(h + 1) * n_pages] + lfn[h:h + 1] for h in hs]
    sl = [slc[h:h + 1] for h in hs]
    m = [jnp.maximum(jnp.max(jnp.max(s_h[h], axis=1, keepdims=True), axis=0, keepdims=True), sl[h])
         for h in hs]
    p_h = [jnp.exp(s_h[h] - m[h]) for h in hs]
    p_self = [jnp.exp(sl[h] - m[h]) for h in hs]
    l = [jnp.sum(jnp.sum(p_h[h], axis=1, keepdims=True), axis=0, keepdims=True) + p_self[h] for h in hs]
    pm = [jnp.max(p_h[h], axis=1, keepdims=True) for h in hs]
    page_max = pm[0]
    for h in hs:
        p_ref[h * n_pages:(h + 1) * n_pages, :] = p_h[h]
        pself_ref[h:h + 1, :] = jnp.broadcast_to(p_self[h], (1, LANES))
        l_ref[h:h + 1, :] = jnp.broadcast_to(l[h], (1, LANES))
        page_max = jnp.maximum(page_max, pm[h])
    hp = pself_ref.shape[0]
    if hp > nh:
        pself_ref[nh:hp, :] = jnp.zeros((hp - nh, LANES), F32)
        l_ref[nh:hp, :] = jnp.ones((hp - nh, LANES), F32)
    alive_ref[...] = jnp.broadcast_to(page_max, (n_pages, LANES))


def _s_fox_values_kernel(pt_ref, p_ref, pself_ref, l_ref, vnT_ref, *rest, nh, n_pages, rows):
    v_pages = rest[0:rows * n_pages]
    tokT_ref = rest[rows * n_pages]
    c = pl.program_id(0)

    @pl.when(c == 0)
    def _():
        tokT_ref[...] = jnp.zeros(tokT_ref.shape, F32)

    lane = lax.broadcasted_iota(jnp.int32, vnT_ref.shape, 1)
    lane_h = lax.broadcasted_iota(jnp.int32, (HEAD_DIM, tokT_ref.shape[1]), 1)
    for r in range(rows):
        b = c * rows + r
        vnc = _column(vnT_ref[...], lane == b)
        sums = []
        for h in range(nh):
            acc = p_ref[r, h * n_pages:h * n_pages + 1, :] * v_pages[r * n_pages][h]
            for p in range(1, n_pages):
                acc = acc + p_ref[r, h * n_pages + p:h * n_pages + p + 1, :] * v_pages[r * n_pages + p][h]
            sums.append(jnp.sum(acc, axis=1, keepdims=True))
        for h in range(nh):
            hd = slice(h * HEAD_DIM, (h + 1) * HEAD_DIM)
            o = (sums[h] + pself_ref[r, h:h + 1, 0:1] * vnc[hd]) / l_ref[r, h:h + 1, 0:1]
            tokT_ref[hd, :] = jnp.where(lane_h == b, o, tokT_ref[hd, :])


def _s_fox(page_table, qT, vnT, slT, lfnT, lf5, k5, v5, fox_layer, *, nh):
    tw, nb = qT.shape
    n_pages = page_table.shape[1]
    page = k5.shape[4]
    hp = slT.shape[0]
    nrow = nh * n_pages
    cparams = pltpu.CompilerParams(dimension_semantics=("arbitrary",), vmem_limit_bytes=VMEM_LIMIT)

    rows = 2 if nb % 2 == 0 else 1
    slots = [(r, p) for r in range(rows) for p in range(n_pages)]
    full = lambda a: pl.BlockSpec(a.shape, lambda c, pt: (0, 0))
    per_step = lambda n, w: pl.BlockSpec((rows, n, w), lambda c, pt: (c, 0, 0))
    kv_specs = [pl.BlockSpec((None, None, nh, HEAD_DIM, page),
                             lambda c, pt, r=r, p=p: (fox_layer, pt[c * rows + r, p], 0, 0, 0)) for r, p in slots]

    lf_specs = [pl.BlockSpec((None, nh, None, 1, page),
                             lambda c, pt, r=r, p=p: (fox_layer, 0, pt[c * rows + r, p], 0, 0)) for r, p in slots]
    probs, p_self, l_sum, page_max = pl.pallas_call(
        functools.partial(_s_fox_scores_kernel, nh=nh, n_pages=n_pages, rows=rows),
        grid_spec=pltpu.PrefetchScalarGridSpec(
            num_scalar_prefetch=1,
            grid=(nb // rows,),
            in_specs=[full(qT), full(slT), full(lfnT)] + lf_specs + kv_specs,
            out_specs=[per_step(nrow, page), per_step(hp, LANES), per_step(hp, LANES),
                       per_step(n_pages, LANES)],
            scratch_shapes=[pltpu.VMEM((rows, nrow, page), F32), pltpu.VMEM((rows, nrow, page), F32)]),
        out_shape=[jax.ShapeDtypeStruct((nb, nrow, page), F32),
                   jax.ShapeDtypeStruct((nb, hp, LANES), F32),
                   jax.ShapeDtypeStruct((nb, hp, LANES), F32),
                   jax.ShapeDtypeStruct((nb, n_pages, LANES), F32)],
        compiler_params=cparams,
        name="sample_fox_scores",
    )(page_table, qT, slT, lfnT, *([lf5] * len(slots)), *([k5] * len(slots)))

    alive = (page_max[:, :, 0] > 0.0).reshape(nb // rows, rows, n_pages)
    step_id = jnp.arange(nb // rows, dtype=jnp.int32)[:, None, None]
    last_live = jnp.maximum(lax.cummax(jnp.where(alive, step_id, -1), axis=0), 0)
    pt_eff = jnp.take_along_axis(page_table.reshape(nb // rows, rows, n_pages), last_live, axis=0)

    return pl.pallas_call(
        functools.partial(_s_fox_values_kernel, nh=nh, n_pages=n_pages, rows=rows),
        grid_spec=pltpu.PrefetchScalarGridSpec(
            num_scalar_prefetch=1,
            grid=(nb // rows,),
            in_specs=[per_step(nrow, page), per_step(hp, LANES), per_step(hp, LANES), full(vnT)] + kv_specs,
            out_specs=pl.BlockSpec((tw, nb), lambda c, pt: (0, 0))),
        out_shape=jax.ShapeDtypeStruct((tw, nb), F32),
        compiler_params=cparams,
        name="sample_fox_values",
    )(pt_eff.reshape(nb, n_pages), probs, p_self, l_sum, vnT, *([v5] * len(slots)))


def _s_post_kernel(xT_ref, tokT_ref, crsT_ref, sgT_ref, wo_ref, g_ref, b_ref, out_ref, *, alpha):
    tw = tokT_ref.shape[0]
    sg = sgT_ref[...]
    mt = (tokT_ref[...] * sg[0:tw]).T.astype(BF16)
    mc = (crsT_ref[...] * sg[tw:]).T.astype(BF16)
    y = alpha * xT_ref[...] + (_dot(mt, wo_ref[0:tw, :]) + _dot(mc, wo_ref[tw:, :])).T
    out_ref[...] = _layer_norm(y, g_ref[...], b_ref[...], axis=0)


def _s_post(xT, tokT, crsT, sgT, woT, gcol, bcol, *, alpha):
    return pl.pallas_call(
        functools.partial(_s_post_kernel, alpha=alpha),
        out_shape=jax.ShapeDtypeStruct(xT.shape, F32),
        compiler_params=pltpu.CompilerParams(vmem_limit_bytes=VMEM_LIMIT),
        name="sample_post",
    )(xT, tokT, crsT, sgT, woT, gcol, bcol)


def kernel(x_prompt, x_sample, mem_prompt, cache_fox_k, cache_fox_v, cache_fox_logf, state_conv,
           cache_mem_k, cache_mem_v, page_table, w_in_conv, conv_w, w_in_fox, b_forget, w_mem_kv,
           w_out, ln_g, ln_b):
    depth = w_out.shape[0]
    alpha = float((2 * depth) ** 0.25)
    bsz, seq, d_model = x_prompt.shape
    nb = x_sample.shape[0]
    tw = conv_w.shape[2]
    nh = tw // HEAD_DIM
    hp = -(-nh // 8) * 8
    tm = min(512, seq)
    tb = min(256, seq)
    bc = min(8, nb)
    kv_all = None

    mkT_all, mvT_all = _mem_kv(mem_prompt, jnp.swapaxes(w_mem_kv, 1, 2).astype(BF16))
    mk5 = jnp.transpose(cache_mem_k, (0, 1, 3, 4, 2))
    mv5 = jnp.transpose(cache_mem_v, (0, 1, 3, 4, 2))
    k5 = jnp.transpose(cache_fox_k, (0, 1, 3, 4, 2))
    v5 = jnp.transpose(cache_fox_v, (0, 1, 3, 4, 2))
    lf5 = jnp.transpose(cache_fox_logf, (0, 3, 1, 2))[:, :, :, None, :]

    xp = x_prompt
    xsT = x_sample.reshape(nb, d_model).T
    fl_p, fk_s, fv_s, fl_s, cv_p, cv_s = [], [], [], [], [], []

    for i in range(depth):
        j = i // 2
        woutb = w_out[i].astype(BF16)
        g_row, b_row = ln_g[i][None, :], ln_b[i][None, :]
        g_col, b_col = ln_g[i][:, None], ln_b[i][:, None]
        if i % 2 == 0:
            wb = w_in_conv[j].astype(BF16)
            xp, st_p = _conv_layer_prompt(xp, wb, conv_w[j], mkT_all[i], mvT_all[i],
                                          woutb, g_row, b_row, alpha=alpha, tm=tm)
            cv_p.append(st_p)
            st = state_conv[j]
            tokT, uT, qmT, sgT = _s_pre_conv(xsT, wb, st[:, 0].T, st[:, 1].T, conv_w[j].T)
            cv_s.append(jnp.stack([st[:, 1], uT.T], axis=1))
            crsT = _s_cross(qmT, mk5, mv5, i, bc=bc)
        else:
            wt = w_in_fox[j].T
            o_f, o_qm, o_g = 3 * tw, 3 * tw + nh, 3 * tw + nh + X_WIDTH
            wT = jnp.concatenate([wt[0:o_f], wt[o_qm:], wt[o_f:o_qm],
                                  jnp.zeros((hp - nh, d_model), F32)], axis=0).astype(BF16)
            bf = jnp.pad(b_forget[j], (0, hp - nh))[:, None]
            q, kT_all, vT_all, kTb, vTb, lfT, d, dTb, jlo, qm, sg = _fox_proj_prompt(
                xp, wT, bf, kv_all, layer=j, n_layers=depth // 2, tm=tm, tb=tb, nh=nh)
            kv_all = (kT_all, vT_all)
            xp = _fox_attn_prompt(jlo, q, kTb, vTb, d, dTb, xp, qm, sg, mkT_all[i], mvT_all[i],
                                  woutb, g_row, b_row, alpha=alpha)
            fl_p.append(jnp.transpose(lfT[:, :nh, :], (0, 2, 1)))

            qT, knT, vnT, lfnT, slT, qmT, sgT = _s_pre_fox(xsT, wT, bf, nh=nh)
            tokT = _s_fox(page_table, qT, vnT, slT, lfnT, lf5, k5, v5, j, nh=nh)
            crsT = _s_cross(qmT, mk5, mv5, i, bc=bc)
            fk_s.append(knT.T.reshape(nb, 1, nh, HEAD_DIM))
            fv_s.append(vnT.T.reshape(nb, 1, nh, HEAD_DIM))
            fl_s.append(lfnT[:nh].T.reshape(nb, 1, nh))
        xsT = _s_post(xsT, tokT, crsT, sgT, woutb, g_col, b_col, alpha=alpha)

    n_mem = mem_prompt.shape[1]
    mk_out = jnp.transpose(mkT_all.reshape(depth, bsz, X_HEADS, HEAD_DIM, n_mem), (0, 1, 4, 2, 3))
    mv_out = jnp.transpose(mvT_all.reshape(depth, bsz, X_HEADS, HEAD_DIM, n_mem), (0, 1, 4, 2, 3))
    n_fox = depth // 2
    fk_all = jnp.transpose(kv_all[0].reshape(n_fox, bsz, nh, HEAD_DIM, seq), (0, 1, 4, 2, 3))
    fv_all = jnp.transpose(kv_all[1].reshape(n_fox, bsz, nh, HEAD_DIM, seq), (0, 1, 4, 2, 3))
    return (xp, xsT.T.reshape(nb, 1, d_model),
            fk_all, fv_all, jnp.stack(fl_p),
            jnp.stack(fk_s), jnp.stack(fv_s), jnp.stack(fl_s),
            jnp.stack(cv_p), jnp.stack(cv_s), mk_out, mv_out)
```

```python
import functools

import jax
import jax.numpy as jnp
from jax import lax
from jax.experimental import pallas as pl
from jax.experimental.pallas import tpu as pltpu

F32 = jnp.float32
BF16 = jnp.bfloat16

HEAD_DIM = 64
X_HEADS = 4
X_WIDTH = X_HEADS * HEAD_DIM
CONV_WIDTH = 3
LN_EPS = 1e-5
QK_SCALE = HEAD_DIM ** -0.5
LOG2E = 1.4426950408889634
SKIP_LOG2 = 150.0
NORM_SLACK = 1.02
MAX_HEAD_GROUP = 6
LANES = 128
VMEM_LIMIT = 56 * 1024 * 1024

_NT = (((1,), (1,)), ((), ()))


def _dot(a, b):
    return jnp.dot(a, b, preferred_element_type=F32)


def _dot_nt(a, b):
    return lax.dot_general(a, b, _NT, preferred_element_type=F32)


def _silu(g):
    return g * jax.nn.sigmoid(g)


def _log_sigmoid(x):
    return jnp.minimum(x, 0.0) - jnp.log1p(jnp.exp(-jnp.abs(x)))


def _layer_norm(y, g, b, axis):
    mu = jnp.mean(y, axis=axis, keepdims=True)
    yc = y - mu
    var = jnp.mean(yc * yc, axis=axis, keepdims=True)
    return yc * lax.rsqrt(var + LN_EPS) * g + b


def _split3(x):
    hi = x.astype(BF16)
    r1 = x - hi.astype(F32)
    mid = r1.astype(BF16)
    lo = (r1 - mid.astype(F32)).astype(BF16)
    return hi, mid, lo


def _const_spec(shape):
    nd = len(shape)
    return pl.BlockSpec(shape, lambda *_: (0,) * nd, pipeline_mode=pl.Buffered(1))


def _mem_kv_kernel(mem_ref, wT_ref, kT_ref, vT_ref):
    memb = mem_ref[0].astype(BF16)
    kvT = _dot_nt(wT_ref[0], memb)
    kT_ref[0, 0] = kvT[:X_WIDTH]
    vT_ref[0, 0] = kvT[X_WIDTH:]


def _mem_kv(mem_prompt, w_mem_kvT):
    depth = w_mem_kvT.shape[0]
    bsz, n_mem, d_model = mem_prompt.shape
    out = jax.ShapeDtypeStruct((depth, bsz, X_WIDTH, n_mem), F32)
    return pl.pallas_call(
        _mem_kv_kernel,
        grid=(depth, bsz),
        in_specs=[pl.BlockSpec((1, n_mem, d_model), lambda i, b: (b, 0, 0)),
                  pl.BlockSpec((1, 2 * X_WIDTH, d_model), lambda i, b: (i, 0, 0))],
        out_specs=[pl.BlockSpec((1, 1, X_WIDTH, n_mem), lambda i, b: (i, b, 0, 0))] * 2,
        out_shape=[out, out],
        compiler_params=pltpu.CompilerParams(dimension_semantics=("arbitrary", "arbitrary")),
        name="mem_kv",
    )(mem_prompt, w_mem_kvT)


def _cross_prompt(qm, kT, vT):
    lane = lax.broadcasted_iota(jnp.int32, qm.shape, 1)
    vrow = lax.broadcasted_iota(jnp.int32, vT.shape, 0)
    out = None
    for h in range(X_HEADS):
        lo, hi = h * HEAD_DIM, (h + 1) * HEAD_DIM
        qh = jnp.where((lane >= lo) & (lane < hi), qm, 0.0).astype(BF16)
        s = _dot(qh, kT)
        m = jnp.max(s, axis=-1, keepdims=True)
        e = jnp.exp(s - m)
        p = e * (1.0 / jnp.sum(e, axis=-1, keepdims=True))
        vh = jnp.where((vrow >= lo) & (vrow < hi), vT, jnp.zeros_like(vT))
        o = _dot_nt(p.astype(BF16), vh)
        out = o if out is None else out + o
    return out


def _finish_prompt(x, tok, crs, sg, wout_ref, g, b, alpha):
    tw = tok.shape[1]
    mt = (tok * sg[:, :tw]).astype(BF16)
    mc = (crs * sg[:, tw:]).astype(BF16)
    y = alpha * x + _dot(mt, wout_ref[0:tw, :]) + _dot(mc, wout_ref[tw:, :])
    return _layer_norm(y, g, b, axis=-1)


def _conv_layer_kernel(x_ref, win_ref, cw_ref, mkT_ref, mvT_ref, wout_ref, g_ref, b_ref,
                       xo_ref, st_ref, ubuf, *, alpha, tw):
    t = pl.program_id(1)
    tm = x_ref.shape[1]

    @pl.when(t == 0)
    def _():
        ubuf[0:8, :] = jnp.zeros((8, tw), F32)

    x = x_ref[0]
    xb = x.astype(BF16)
    bg = _dot(xb, win_ref[:, 0:tw])
    u = _dot(xb, win_ref[:, tw:2 * tw]) * _dot(xb, win_ref[:, 2 * tw:3 * tw])
    ubuf[8:8 + tm, :] = u
    cw = cw_ref[...]
    y = cw[0:1] * ubuf[6:6 + tm, :] + cw[1:2] * ubuf[7:7 + tm, :] + cw[2:3] * u
    tok = bg * y
    last = ubuf[tm + 6:tm + 8, :]
    ubuf[6:8, :] = last
    st_ref[0] = last

    qm = _dot(xb, win_ref[:, 3 * tw:3 * tw + X_WIDTH]) * QK_SCALE
    gate = _dot(xb, win_ref[:, 3 * tw + X_WIDTH:])
    crs = _cross_prompt(qm, mkT_ref[0].astype(BF16), mvT_ref[0].astype(BF16))
    xo_ref[0] = _finish_prompt(x, tok, crs, _silu(gate), wout_ref, g_ref[...], b_ref[...], alpha)


def _conv_layer_prompt(x, win, cw, mkT, mvT, wout, g, b, *, alpha, tm):
    bsz, seq, d_model = x.shape
    tw = cw.shape[1]
    n_mem = mkT.shape[2]
    kern = functools.partial(_conv_layer_kernel, alpha=alpha, tw=tw)
    return pl.pallas_call(
        kern,
        grid=(bsz, seq // tm),
        in_specs=[pl.BlockSpec((1, tm, d_model), lambda bb, t: (bb, t, 0)),
                  _const_spec(win.shape),
                  _const_spec(cw.shape),
                  pl.BlockSpec((1, X_WIDTH, n_mem), lambda bb, t: (bb, 0, 0)),
                  pl.BlockSpec((1, X_WIDTH, n_mem), lambda bb, t: (bb, 0, 0)),
                  _const_spec(wout.shape),
                  _const_spec(g.shape),
                  _const_spec(b.shape)],
        out_specs=[pl.BlockSpec((1, tm, d_model), lambda bb, t: (bb, t, 0)),
                   pl.BlockSpec((1, CONV_WIDTH - 1, tw), lambda bb, t: (bb, 0, 0))],
        out_shape=[jax.ShapeDtypeStruct(x.shape, F32),
                   jax.ShapeDtypeStruct((bsz, CONV_WIDTH - 1, tw), F32)],
        scratch_shapes=[pltpu.VMEM((tm + 8, tw), F32)],
        compiler_params=pltpu.CompilerParams(dimension_semantics=("arbitrary", "arbitrary"),
                                             vmem_limit_bytes=VMEM_LIMIT),
        name="conv_layer_prompt",
    )(x, win, cw, mkT, mvT, wout, g, b)


def _fox_proj_kernel(x_ref, wT_ref, bf_ref, *rest, tw, nh, tb, n_alias):
    (q_ref, kT_ref, vT_ref, kTb_ref, vTb_ref, lfT_ref, d_ref, dTb_ref, jlo_ref, qm_ref, sg_ref,
     carry_ref, tri_ref, hk_ref, hd_ref) = rest[n_alias:]
    t = pl.program_id(1)
    tm = x_ref.shape[1]
    d_model = x_ref.shape[2]
    hp = bf_ref.shape[0]
    o_qm, o_g, o_f = 3 * tw, 3 * tw + X_WIDTH, 3 * tw + X_WIDTH + d_model
    nsub = tm // tb

    @pl.when(t == 0)
    def _():
        carry_ref[...] = jnp.zeros(carry_ref.shape, F32)
        hk_ref[...] = jnp.zeros(hk_ref.shape, F32)
        hd_ref[...] = jnp.zeros(hd_ref.shape, F32)
        r = lax.broadcasted_iota(jnp.int32, (tm, tm), 0)
        c = lax.broadcasted_iota(jnp.int32, (tm, tm), 1)
        tri_ref[...] = jnp.where(r <= c, 1.0, 0.0).astype(BF16)

    xb = x_ref[0].astype(BF16)
    q2 = (_dot_nt(xb, wT_ref[0:tw, :]) * (QK_SCALE * LOG2E)).astype(BF16)
    q_ref[0] = q2
    qm_ref[0] = (_dot_nt(xb, wT_ref[o_qm:o_g, :]) * QK_SCALE).astype(BF16)
    sg_ref[0] = _silu(_dot_nt(xb, wT_ref[o_g:o_f, :])).astype(BF16)

    kT = _dot_nt(wT_ref[tw:2 * tw, :], xb)
    vT = _dot_nt(wT_ref[2 * tw:3 * tw, :], xb)
    kT_ref[0] = kT
    vT_ref[0] = vT
    kTb = kT.astype(BF16)
    vTb = vT.astype(BF16)
    for u in range(nsub):
        for h in range(nh):
            kTb_ref[0, u, h] = kTb[h * HEAD_DIM:(h + 1) * HEAD_DIM, u * tb:(u + 1) * tb]
            vTb_ref[0, u, h] = vTb[h * HEAD_DIM:(h + 1) * HEAD_DIM, u * tb:(u + 1) * tb]

    fT = _dot_nt(wT_ref[o_f:o_f + hp, :], xb) + bf_ref[...]
    hrow = lax.broadcasted_iota(jnp.int32, (hp, tm), 0)
    lf = jnp.where(hrow < nh, _log_sigmoid(fT), 0.0)
    lfT_ref[0] = lf

    cs3 = _dot(jnp.concatenate(_split3(lf), axis=0), tri_ref[...])
    cs = cs3[0:hp] + cs3[hp:2 * hp] + cs3[2 * hp:3 * hp] + carry_ref[:, 0:1]
    carry_ref[...] = jnp.broadcast_to(cs[:, tm - 1:tm], carry_ref.shape)
    cs2 = cs * LOG2E
    for u in range(nsub):
        for h in range(hp):
            dTb_ref[0, u, h] = cs2[h:h + 1, u * tb:(u + 1) * tb]
    d3 = [t.astype(F32) for t in _split3(cs2)] + [jnp.zeros((LANES - 3 * hp, tm), F32)]
    d_ref[0] = jnp.concatenate(d3, axis=0).T.astype(BF16)

    head_of_col = lax.broadcasted_iota(jnp.int32, (LANES, tw), 1) // HEAD_DIM
    sel = jnp.where(head_of_col == lax.broadcasted_iota(jnp.int32, (LANES, tw), 0), 1.0, 0.0).astype(BF16)
    q2f = q2.astype(F32)
    nq = _dot_nt(sel, (q2f * q2f).astype(BF16))
    kf = kTb.astype(F32)
    ksq = kf * kf
    nk = jnp.concatenate([jnp.sum(ksq[h * HEAD_DIM:(h + 1) * HEAD_DIM], axis=0, keepdims=True)
                          for h in range(nh)] + [jnp.zeros((hp - nh, tm), F32)], axis=0)
    lane = lax.broadcasted_iota(jnp.int32, (hp, LANES), 1)
    lane_f = lane.astype(F32)
    for u in range(nsub):
        i = t * nsub + u
        qmax = jnp.sqrt(jnp.max(nq[0:hp, u * tb:(u + 1) * tb], axis=1, keepdims=True))
        kmax = jnp.sqrt(jnp.max(nk[:, u * tb:(u + 1) * tb], axis=1, keepdims=True))
        hk = jnp.where(lane == i, kmax, hk_ref[...])
        hd = jnp.where(lane == i, cs2[:, (u + 1) * tb - 1:(u + 1) * tb], hd_ref[...])
        hk_ref[...] = hk
        hd_ref[...] = hd
        gap = qmax * (hk + kmax) * NORM_SLACK + cs2[:, u * tb:u * tb + 1] - hd
        alive = (gap >= -SKIP_LOG2) & (lane < i)
        i_f = i.astype(F32)
        jlo_ref[0, u] = jnp.min(jnp.where(alive, lane_f, i_f), axis=1, keepdims=True).astype(jnp.int32)


def _fox_proj_prompt(x, wT, bf, kv_prev, *, layer, n_layers, tm, tb, nh):
    bsz, seq, d_model = x.shape
    tw = nh * HEAD_DIM
    hp = bf.shape[0]
    nblk = seq // tb
    nsub = tm // tb
    assert nblk <= LANES
    aliased = [] if kv_prev is None else list(kv_prev)
    kern = functools.partial(_fox_proj_kernel, tw=tw, nh=nh, tb=tb, n_alias=len(aliased))
    row = lambda bb, t: (bb, t, 0)
    colT = lambda bb, t: (bb, 0, t)
    blocked = lambda bb, t: (bb, t, 0, 0, 0)
    kv_all = jax.ShapeDtypeStruct((n_layers, bsz, tw, seq), F32)
    kv_spec = pl.BlockSpec((None, 1, tw, tm), lambda bb, t: (layer, bb, 0, t))
    out_shape = [
        jax.ShapeDtypeStruct((bsz, seq, tw), BF16),
        kv_all,
        kv_all,
        jax.ShapeDtypeStruct((bsz, nblk, nh, HEAD_DIM, tb), BF16),
        jax.ShapeDtypeStruct((bsz, nblk, nh, HEAD_DIM, tb), BF16),
        jax.ShapeDtypeStruct((bsz, hp, seq), F32),
        jax.ShapeDtypeStruct((bsz, seq, LANES), BF16),
        jax.ShapeDtypeStruct((bsz, nblk, hp, 1, tb), F32),
        jax.ShapeDtypeStruct((bsz, nblk, hp, 1), jnp.int32),
        jax.ShapeDtypeStruct((bsz, seq, X_WIDTH), BF16),
        jax.ShapeDtypeStruct((bsz, seq, d_model), BF16),
    ]
    out_specs = [
        pl.BlockSpec((1, tm, tw), row),
        kv_spec,
        kv_spec,
        pl.BlockSpec((1, nsub, nh, HEAD_DIM, tb), blocked),
        pl.BlockSpec((1, nsub, nh, HEAD_DIM, tb), blocked),
        pl.BlockSpec((1, hp, tm), colT),
        pl.BlockSpec((1, tm, LANES), row),
        pl.BlockSpec((1, nsub, hp, 1, tb), blocked),
        pl.BlockSpec((1, nsub, hp, 1), lambda bb, t: (bb, t, 0, 0)),
        pl.BlockSpec((1, tm, X_WIDTH), row),
        pl.BlockSpec((1, tm, d_model), row),
    ]
    n_in = 3
    return pl.pallas_call(
        kern,
        grid=(bsz, seq // tm),
        in_specs=[pl.BlockSpec((1, tm, d_model), row), _const_spec(wT.shape),
                  _const_spec(bf.shape)] + [pl.BlockSpec(memory_space=pl.ANY)] * len(aliased),
        out_specs=out_specs,
        out_shape=out_shape,
        input_output_aliases={n_in + k: 1 + k for k in range(len(aliased))},
        scratch_shapes=[pltpu.VMEM((hp, LANES), F32), pltpu.VMEM((tm, tm), BF16),
                        pltpu.VMEM((hp, LANES), F32), pltpu.VMEM((hp, LANES), F32)],
        compiler_params=pltpu.CompilerParams(dimension_semantics=("arbitrary", "arbitrary"),
                                             vmem_limit_bytes=VMEM_LIMIT),
        name="fox_proj_prompt",
    )(x, wT, bf, *aliased)


def _fox_attn_kernel(jlo_ref, q_ref, kTb_ref, vTb_ref, d_ref, dTb_ref, sg_ref, mt_ref,
                     qh_scr, dq_scr, m_scr, l_scr, acc_scr, tok_scr, *, nh, hp, group):
    bb = pl.program_id(0)
    i = pl.program_id(1)
    nblk = pl.num_programs(1)
    tq = q_ref.shape[1]
    tk = kTb_ref.shape[4]
    reps = tk // LANES

    d3 = d_ref[0]
    term = lax.broadcasted_iota(jnp.int32, (LANES, LANES), 0)
    for h in range(nh):
        qh_scr[h] = q_ref[0, :, h * HEAD_DIM:(h + 1) * HEAD_DIM]
        pick = (term == h) | (term == hp + h) | (term == 2 * hp + h)
        dq_scr[h] = _dot(d3, jnp.where(pick, 1.0, 0.0).astype(BF16))

    row = lax.broadcasted_iota(jnp.int32, (tq, tk), 0)
    col = lax.broadcasted_iota(jnp.int32, (tq, tk), 1)
    causal = col <= row

    def group_body(g, carry):
        heads = [g * group + k for k in range(group)]
        m_scr[...] = jnp.full(m_scr.shape, -jnp.inf, F32)
        l_scr[...] = jnp.zeros(l_scr.shape, F32)
        acc_scr[...] = jnp.zeros(acc_scr.shape, F32)
        qs = [qh_scr[h] for h in heads]
        dqts = [jnp.concatenate([dq_scr[h]] * reps, axis=1) for h in heads]

        def step(j, masked):
            for k, h in enumerate(heads):
                s = _dot(qs[k], kTb_ref[0, j, h])
                s = s + dqts[k] - dTb_ref[0, j, h]
                if masked:
                    s = jnp.where(causal, s, -jnp.inf)
                m_old = m_scr[k]
                m_new = jnp.maximum(m_old, jnp.max(s, axis=1, keepdims=True))
                a = jnp.exp2(m_old - m_new)
                p = jnp.exp2(s - jnp.concatenate([m_new] * reps, axis=1))
                l_scr[k] = a * l_scr[k] + jnp.sum(p, axis=1, keepdims=True)
                m_scr[k] = m_new
                pv = _dot_nt(p.astype(BF16), vTb_ref[0, j, h])
                acc_scr[k] = a[:, 0:HEAD_DIM] * acc_scr[k] + pv

        def loop_body(j, c):
            step(j, False)
            return c

        base = (bb * nblk + i) * hp
        j0 = jlo_ref[base + heads[0]]
        for h in heads[1:]:
            j0 = jnp.minimum(j0, jlo_ref[base + h])
        lax.fori_loop(j0, i, loop_body, 0)
        step(i, True)
        for k, h in enumerate(heads):
            tok_scr[h] = acc_scr[k] / l_scr[k][:, 0:HEAD_DIM]
        return carry

    lax.fori_loop(0, nh // group, group_body, 0)

    tok = jnp.concatenate([tok_scr[h] for h in range(nh)], axis=1)
    mt_ref[0] = (tok * sg_ref[0][:, 0:nh * HEAD_DIM].astype(F32)).astype(BF16)


def _fox_finish_kernel(x_ref, mt_ref, qm_ref, sg_ref, mkT_ref, mvT_ref, wout_ref, g_ref, b_ref, xo_ref,
                       *, alpha):
    tw = mt_ref.shape[2]
    crs = _cross_prompt(qm_ref[0].astype(F32), mkT_ref[0].astype(BF16), mvT_ref[0].astype(BF16))
    mc = (crs * sg_ref[0][:, tw:].astype(F32)).astype(BF16)
    y = alpha * x_ref[0] + _dot(mt_ref[0], wout_ref[0:tw, :]) + _dot(mc, wout_ref[tw:, :])
    xo_ref[0] = _layer_norm(y, g_ref[...], b_ref[...], axis=-1)


def _fox_finish_prompt(x, mt, qm, sg, mkT, mvT, wout, g, b, *, alpha, tm):
    bsz, seq, d_model = x.shape
    n_mem = mkT.shape[2]
    row = lambda w: pl.BlockSpec((1, tm, w), lambda bb, t: (bb, t, 0))
    per_b = pl.BlockSpec((1, X_WIDTH, n_mem), lambda bb, t: (bb, 0, 0))
    return pl.pallas_call(
        functools.partial(_fox_finish_kernel, alpha=alpha),
        grid=(bsz, seq // tm),
        in_specs=[row(d_model), row(mt.shape[2]), row(X_WIDTH), row(d_model), per_b, per_b,
                  _const_spec(wout.shape), _const_spec(g.shape), _const_spec(b.shape)],
        out_specs=row(d_model),
        out_shape=jax.ShapeDtypeStruct(x.shape, F32),
        compiler_params=pltpu.CompilerParams(dimension_semantics=("arbitrary", "arbitrary"),
                                             vmem_limit_bytes=VMEM_LIMIT),
        name="fox_finish_prompt",
    )(x, mt, qm, sg, mkT, mvT, wout, g, b)


def _fox_attn_prompt(jlo, q, kTb, vTb, d, dTb, sg):
    bsz, seq, d_model = sg.shape
    _, nblk, nh, _, tk = kTb.shape
    tq = tk
    tw = nh * HEAD_DIM
    hp = dTb.shape[2]
    group = max(k for k in range(1, MAX_HEAD_GROUP + 1) if nh % k == 0)
    kern = functools.partial(_fox_attn_kernel, nh=nh, hp=hp, group=group)
    row = lambda bb, t, jl: (bb, t, 0)
    per_b5 = lambda bb, t, jl: (bb, 0, 0, 0, 0)

    grid_spec = pltpu.PrefetchScalarGridSpec(
        num_scalar_prefetch=1,
        grid=(bsz, nblk),
        in_specs=[pl.BlockSpec((1, tq, tw), row),
                  pl.BlockSpec((1, nblk, nh, HEAD_DIM, tk), per_b5, pipeline_mode=pl.Buffered(1)),
                  pl.BlockSpec((1, nblk, nh, HEAD_DIM, tk), per_b5, pipeline_mode=pl.Buffered(1)),
                  pl.BlockSpec((1, tq, LANES), row),
                  pl.BlockSpec((1, nblk, hp, 1, tk), per_b5, pipeline_mode=pl.Buffered(1)),
                  pl.BlockSpec((1, tq, d_model), row)],
        out_specs=pl.BlockSpec((1, tq, tw), row),
        scratch_shapes=[pltpu.VMEM((nh, tq, HEAD_DIM), BF16),
                        pltpu.VMEM((nh, tq, LANES), F32),
                        pltpu.VMEM((group, tq, LANES), F32),
                        pltpu.VMEM((group, tq, LANES), F32),
                        pltpu.VMEM((group, tq, HEAD_DIM), F32),
                        pltpu.VMEM((nh, tq, HEAD_DIM), F32)],
    )
    return pl.pallas_call(
        kern,
        grid_spec=grid_spec,
        out_shape=jax.ShapeDtypeStruct((bsz, seq, tw), BF16),
        compiler_params=pltpu.CompilerParams(dimension_semantics=("arbitrary", "arbitrary"),
                                             vmem_limit_bytes=VMEM_LIMIT),
        name="fox_attn_prompt",
    )(jlo.reshape(-1), q, kTb, vTb, d, dTb, sg)


def _s_pre_conv_kernel(xT_ref, w_ref, st0_ref, st1_ref, cw_ref,
                       tokT_ref, uT_ref, qmT_ref, sgT_ref, *, tw):
    xb = xT_ref[...].T.astype(BF16)
    proj = lambda lo, hi: _dot(xb, w_ref[:, lo:hi]).T
    bg = proj(0, tw)
    u = proj(tw, 2 * tw) * proj(2 * tw, 3 * tw)
    cw = cw_ref[...]
    y = cw[:, 0:1] * st0_ref[...] + cw[:, 1:2] * st1_ref[...] + cw[:, 2:3] * u
    tokT_ref[...] = bg * y
    uT_ref[...] = u
    qmT_ref[...] = proj(3 * tw, 3 * tw + X_WIDTH) * QK_SCALE
    sgT_ref[...] = _silu(proj(3 * tw + X_WIDTH, w_ref.shape[1]))


def _s_pre_conv(xT, wT, st0T, st1T, cwT):
    d_model, nb = xT.shape
    tw = cwT.shape[0]
    f = lambda r: jax.ShapeDtypeStruct((r, nb), F32)
    return pl.pallas_call(
        functools.partial(_s_pre_conv_kernel, tw=tw),
        out_shape=[f(tw), f(tw), f(X_WIDTH), f(d_model)],
        compiler_params=pltpu.CompilerParams(vmem_limit_bytes=VMEM_LIMIT),
        name="sample_pre_conv",
    )(xT, wT, st0T, st1T, cwT)


def _s_pre_fox_kernel(xT_ref, wT_ref, bf_ref,
                      qT_ref, kT_ref, vT_ref, lfT_ref, slT_ref, qmT_ref, sgT_ref, *, tw, nh):
    xb = xT_ref[...].astype(BF16)
    nb = xb.shape[1]
    hp = bf_ref.shape[0]
    q = _dot(wT_ref[0:tw, :], xb) * QK_SCALE
    k = _dot(wT_ref[tw:2 * tw, :], xb)
    qT_ref[...] = q
    kT_ref[...] = k
    vT_ref[...] = _dot(wT_ref[2 * tw:3 * tw, :], xb)
    qmT_ref[...] = _dot(wT_ref[3 * tw:3 * tw + X_WIDTH, :], xb) * QK_SCALE
    o = 3 * tw + X_WIDTH
    d_model = xb.shape[0]
    sgT_ref[...] = _silu(_dot(wT_ref[o:o + d_model, :], xb))
    fT = _dot(wT_ref[o + d_model:, :], xb) + bf_ref[...]
    hrow = lax.broadcasted_iota(jnp.int32, (hp, nb), 0)
    lfT_ref[...] = jnp.where(hrow < nh, _log_sigmoid(fT), 0.0)
    qk = q * k
    rows = [jnp.sum(qk[h * HEAD_DIM:(h + 1) * HEAD_DIM], axis=0, keepdims=True) for h in range(nh)]
    rows.append(jnp.zeros((hp - nh, nb), F32))
    slT_ref[...] = jnp.concatenate(rows, axis=0)


def _s_pre_fox(xT, wT, bf, *, nh):
    d_model, nb = xT.shape
    tw = nh * HEAD_DIM
    hp = bf.shape[0]
    f = lambda r: jax.ShapeDtypeStruct((r, nb), F32)
    return pl.pallas_call(
        functools.partial(_s_pre_fox_kernel, tw=tw, nh=nh),
        out_shape=[f(tw), f(tw), f(tw), f(hp), f(hp), f(X_WIDTH), f(d_model)],
        compiler_params=pltpu.CompilerParams(vmem_limit_bytes=VMEM_LIMIT),
        name="sample_pre_fox",
    )(xT, wT, bf)


def _column(ref_or_val, onehot):
    return jnp.sum(jnp.where(onehot, ref_or_val, 0.0), axis=1, keepdims=True)


def _cross_sample_cols(qcs, mks, mvs):
    pairs = [(i, h) for i in range(len(qcs)) for h in range(X_HEADS)]
    hd = lambda h: slice(h * HEAD_DIM, (h + 1) * HEAD_DIM)
    s = [jnp.sum(qcs[i][hd(h)] * mks[i][h], axis=0, keepdims=True) for i, h in pairs]
    m = [jnp.max(x, axis=1, keepdims=True) for x in s]
    e = [jnp.exp(x - y) for x, y in zip(s, m)]
    r = [1.0 / jnp.sum(x, axis=1, keepdims=True) for x in e]
    p = [x * y for x, y in zip(e, r)]
    o = [jnp.sum(x * mvs[i][h], axis=1, keepdims=True) for x, (i, h) in zip(p, pairs)]
    return [jnp.concatenate(o[i * X_HEADS:(i + 1) * X_HEADS], axis=0) for i in range(len(qcs))]


def _s_cross_kernel(qmT_ref, mk_ref, mv_ref, crsT_ref, *, bc):
    c = pl.program_id(0)

    @pl.when(c == 0)
    def _():
        crsT_ref[...] = jnp.zeros(crsT_ref.shape, F32)

    lane = lax.broadcasted_iota(jnp.int32, crsT_ref.shape, 1)
    qmT = qmT_ref[...]
    hots = [lane == c * bc + bb for bb in range(bc)]
    cols = _cross_sample_cols([_column(qmT, hot) for hot in hots],
                              [mk_ref.at[bb] for bb in range(bc)], [mv_ref.at[bb] for bb in range(bc)])
    out = crsT_ref[...]
    for bb in range(bc):
        out = jnp.where(hots[bb], cols[bb], out)
    crsT_ref[...] = out


def _s_cross(qmT, mk5, mv5, layer, *, bc):
    nb = qmT.shape[1]
    n_mem = mk5.shape[4]
    blk = pl.BlockSpec((None, bc, X_HEADS, HEAD_DIM, n_mem), lambda c: (layer, c, 0, 0, 0))
    return pl.pallas_call(
        functools.partial(_s_cross_kernel, bc=bc),
        grid=(nb // bc,),
        in_specs=[pl.BlockSpec(qmT.shape, lambda c: (0, 0)), blk, blk],
        out_specs=pl.BlockSpec(qmT.shape, lambda c: (0, 0)),
        out_shape=jax.ShapeDtypeStruct(qmT.shape, F32),
        compiler_params=pltpu.CompilerParams(dimension_semantics=("arbitrary",),
                                             vmem_limit_bytes=VMEM_LIMIT),
        name="sample_cross",
    )(qmT, mk5, mv5)


def _s_fox_scores_kernel(pt_ref, qT_ref, slT_ref, lfnT_ref, *rest, nh, n_pages, rows):
    n = rows * n_pages
    p_ref, pself_ref, l_ref, alive_ref, l_scr, s_scr = rest[2 * n:]
    for r in range(rows):
        _s_fox_scores_row(pl.program_id(0) * rows + r, qT_ref, slT_ref, lfnT_ref,
                          rest[r * n_pages:(r + 1) * n_pages], rest[n + r * n_pages:n + (r + 1) * n_pages],
                          p_ref.at[r], pself_ref.at[r], l_ref.at[r], alive_ref.at[r],
                          l_scr.at[r], s_scr.at[r], nh=nh, n_pages=n_pages)


def _s_fox_scores_row(b, qT_ref, slT_ref, lfnT_ref, lf_pages, k_pages,
                      p_ref, pself_ref, l_ref, alive_ref, l_scr, s_scr, *, nh, n_pages):
    page = k_pages[0].shape[2]
    nrow = nh * n_pages

    def onehot(shape):
        return lax.broadcasted_iota(jnp.int32, shape, 1) == b

    qc = _column(qT_ref[...], onehot(qT_ref.shape))
    slc = _column(slT_ref[...], onehot(slT_ref.shape))
    lfn = _column(lfnT_ref[...], onehot(lfnT_ref.shape))

    for p in range(n_pages):
        for h in range(nh):
            l_scr[h * n_pages + p:h * n_pages + p + 1, :] = lf_pages[p][h]
    lfp = l_scr[...]
    r = lax.broadcasted_iota(jnp.int32, (page, page), 0)
    c = lax.broadcasted_iota(jnp.int32, (page, page), 1)
    after_in_page = jnp.where(r > c, 1.0, 0.0).astype(BF16)
    w3 = _dot(jnp.concatenate(_split3(lfp), axis=0), after_in_page)
    within = w3[0:nrow] + w3[nrow:2 * nrow] + w3[2 * nrow:3 * nrow]
    tot = jnp.broadcast_to(jnp.sum(lfp, axis=1, keepdims=True), (nrow, page))
    ru = lax.broadcasted_iota(jnp.int32, (nrow, nrow), 0)
    cu = lax.broadcasted_iota(jnp.int32, (nrow, nrow), 1)
    later_page = jnp.where((cu > ru) & (cu // n_pages == ru // n_pages), 1.0, 0.0).astype(BF16)
    a3 = _dot(later_page, jnp.concatenate(_split3(tot), axis=1))
    bias = within + a3[:, 0:page] + a3[:, page:2 * page] + a3[:, 2 * page:3 * page]

    for h in range(nh):
        qh = qc[h * HEAD_DIM:(h + 1) * HEAD_DIM]
        for p in range(n_pages):
            s_scr[h * n_pages + p:h * n_pages + p + 1, :] = jnp.sum(qh * k_pages[p][h], axis=0, keepdims=True)
    s_all = s_scr[...] + bias

    hs = range(nh)
    s_h = [s_all[h * n_pages:(h + 1) * n_pages] + lfn[h:h + 1] for h in hs]
    sl = [slc[h:h + 1] for h in hs]
    m = [jnp.maximum(jnp.max(jnp.max(s_h[h], axis=1, keepdims=True), axis=0, keepdims=True), sl[h])
         for h in hs]
    p_h = [jnp.exp(s_h[h] - m[h]) for h in hs]
    p_self = [jnp.exp(sl[h] - m[h]) for h in hs]
    l = [jnp.sum(jnp.sum(p_h[h], axis=1, keepdims=True), axis=0, keepdims=True) + p_self[h] for h in hs]
    pm = [jnp.max(p_h[h], axis=1, keepdims=True) for h in hs]
    page_max = pm[0]
    for h in hs:
        p_ref[h * n_pages:(h + 1) * n_pages, :] = p_h[h]
        pself_ref[h:h + 1, :] = jnp.broadcast_to(p_self[h], (1, LANES))
        l_ref[h:h + 1, :] = jnp.broadcast_to(l[h], (1, LANES))
        page_max = jnp.maximum(page_max, pm[h])
    hp = pself_ref.shape[0]
    if hp > nh:
        pself_ref[nh:hp, :] = jnp.zeros((hp - nh, LANES), F32)
        l_ref[nh:hp, :] = jnp.ones((hp - nh, LANES), F32)
    alive_ref[...] = jnp.broadcast_to(page_max, (n_pages, LANES))


def _s_fox_values_kernel(pt_ref, p_ref, pself_ref, l_ref, vnT_ref, *rest, nh, n_pages, rows):
    v_pages = rest[0:rows * n_pages]
    tokT_ref = rest[rows * n_pages]
    c = pl.program_id(0)

    @pl.when(c == 0)
    def _():
        tokT_ref[...] = jnp.zeros(tokT_ref.shape, F32)

    lane = lax.broadcasted_iota(jnp.int32, vnT_ref.shape, 1)
    lane_h = lax.broadcasted_iota(jnp.int32, (HEAD_DIM, tokT_ref.shape[1]), 1)
    for r in range(rows):
        b = c * rows + r
        vnc = _column(vnT_ref[...], lane == b)
        sums = []
        for h in range(nh):
            acc = p_ref[r, h * n_pages:h * n_pages + 1, :] * v_pages[r * n_pages][h]
            for p in range(1, n_pages):
                acc = acc + p_ref[r, h * n_pages + p:h * n_pages + p + 1, :] * v_pages[r * n_pages + p][h]
            sums.append(jnp.sum(acc, axis=1, keepdims=True))
        for h in range(nh):
            hd = slice(h * HEAD_DIM, (h + 1) * HEAD_DIM)
            o = (sums[h] + pself_ref[r, h:h + 1, 0:1] * vnc[hd]) / l_ref[r, h:h + 1, 0:1]
            tokT_ref[hd, :] = jnp.where(lane_h == b, o, tokT_ref[hd, :])


def _s_fox(page_table, qT, vnT, slT, lfnT, lf5, k5, v5, fox_layer, *, nh):
    tw, nb = qT.shape
    n_pages = page_table.shape[1]
    page = k5.shape[4]
    hp = slT.shape[0]
    nrow = nh * n_pages
    cparams = pltpu.CompilerParams(dimension_semantics=("arbitrary",), vmem_limit_bytes=VMEM_LIMIT)

    rows = 2 if nb % 2 == 0 else 1
    slots = [(r, p) for r in range(rows) for p in range(n_pages)]
    full = lambda a: pl.BlockSpec(a.shape, lambda c, pt: (0, 0))
    per_step = lambda n, w: pl.BlockSpec((rows, n, w), lambda c, pt: (c, 0, 0))
    kv_specs = [pl.BlockSpec((None, None, nh, HEAD_DIM, page),
                             lambda c, pt, r=r, p=p: (fox_layer, pt[c * rows + r, p], 0, 0, 0)) for r, p in slots]

    lf_specs = [pl.BlockSpec((None, nh, None, 1, page),
                             lambda c, pt, r=r, p=p: (fox_layer, 0, pt[c * rows + r, p], 0, 0)) for r, p in slots]
    probs, p_self, l_sum, page_max = pl.pallas_call(
        functools.partial(_s_fox_scores_kernel, nh=nh, n_pages=n_pages, rows=rows),
        grid_spec=pltpu.PrefetchScalarGridSpec(
            num_scalar_prefetch=1,
            grid=(nb // rows,),
            in_specs=[full(qT), full(slT), full(lfnT)] + lf_specs + kv_specs,
            out_specs=[per_step(nrow, page), per_step(hp, LANES), per_step(hp, LANES),
                       per_step(n_pages, LANES)],
            scratch_shapes=[pltpu.VMEM((rows, nrow, page), F32), pltpu.VMEM((rows, nrow, page), F32)]),
        out_shape=[jax.ShapeDtypeStruct((nb, nrow, page), F32),
                   jax.ShapeDtypeStruct((nb, hp, LANES), F32),
                   jax.ShapeDtypeStruct((nb, hp, LANES), F32),
                   jax.ShapeDtypeStruct((nb, n_pages, LANES), F32)],
        compiler_params=cparams,
        name="sample_fox_scores",
    )(page_table, qT, slT, lfnT, *([lf5] * len(slots)), *([k5] * len(slots)))

    alive = (page_max[:, :, 0] > 0.0).reshape(nb // rows, rows, n_pages)
    step_id = jnp.arange(nb // rows, dtype=jnp.int32)[:, None, None]
    last_live = jnp.maximum(lax.cummax(jnp.where(alive, step_id, -1), axis=0), 0)
    pt_eff = jnp.take_along_axis(page_table.reshape(nb // rows, rows, n_pages), last_live, axis=0)

    return pl.pallas_call(
        functools.partial(_s_fox_values_kernel, nh=nh, n_pages=n_pages, rows=rows),
        grid_spec=pltpu.PrefetchScalarGridSpec(
            num_scalar_prefetch=1,
            grid=(nb // rows,),
            in_specs=[per_step(nrow, page), per_step(hp, LANES), per_step(hp, LANES), full(vnT)] + kv_specs,
            out_specs=pl.BlockSpec((tw, nb), lambda c, pt: (0, 0))),
        out_shape=jax.ShapeDtypeStruct((tw, nb), F32),
        compiler_params=cparams,
        name="sample_fox_values",
    )(pt_eff.reshape(nb, n_pages), probs, p_self, l_sum, vnT, *([v5] * len(slots)))


def _s_post_kernel(xT_ref, tokT_ref, crsT_ref, sgT_ref, wo_ref, g_ref, b_ref, out_ref, *, alpha):
    tw = tokT_ref.shape[0]
    sg = sgT_ref[...]
    mt = (tokT_ref[...] * sg[0:tw]).T.astype(BF16)
    mc = (crsT_ref[...] * sg[tw:]).T.astype(BF16)
    y = alpha * xT_ref[...] + (_dot(mt, wo_ref[0:tw, :]) + _dot(mc, wo_ref[tw:, :])).T
    out_ref[...] = _layer_norm(y, g_ref[...], b_ref[...], axis=0)


def _s_post(xT, tokT, crsT, sgT, woT, gcol, bcol, *, alpha):
    return pl.pallas_call(
        functools.partial(_s_post_kernel, alpha=alpha),
        out_shape=jax.ShapeDtypeStruct(xT.shape, F32),
        compiler_params=pltpu.CompilerParams(vmem_limit_bytes=VMEM_LIMIT),
        name="sample_post",
    )(xT, tokT, crsT, sgT, woT, gcol, bcol)


def kernel(x_prompt, x_sample, mem_prompt, cache_fox_k, cache_fox_v, cache_fox_logf, state_conv,
           cache_mem_k, cache_mem_v, page_table, w_in_conv, conv_w, w_in_fox, b_forget, w_mem_kv,
           w_out, ln_g, ln_b):
    depth = w_out.shape[0]
    alpha = float((2 * depth) ** 0.25)
    bsz, seq, d_model = x_prompt.shape
    nb = x_sample.shape[0]
    tw = conv_w.shape[2]
    nh = tw // HEAD_DIM
    hp = -(-nh // 8) * 8
    tm = min(1024, seq)
    tb = min(256, seq)
    bc = min(8, nb)
    kv_all = None

    mkT_all, mvT_all = _mem_kv(mem_prompt, jnp.swapaxes(w_mem_kv, 1, 2).astype(BF16))
    mk5 = jnp.transpose(cache_mem_k, (0, 1, 3, 4, 2))
    mv5 = jnp.transpose(cache_mem_v, (0, 1, 3, 4, 2))
    k5 = jnp.transpose(cache_fox_k, (0, 1, 3, 4, 2))
    v5 = jnp.transpose(cache_fox_v, (0, 1, 3, 4, 2))
    lf5 = jnp.transpose(cache_fox_logf, (0, 3, 1, 2))[:, :, :, None, :]

    xp = x_prompt
    xsT = x_sample.reshape(nb, d_model).T
    fl_p, fk_s, fv_s, fl_s, cv_p, cv_s = [], [], [], [], [], []

    for i in range(depth):
        j = i // 2
        woutb = w_out[i].astype(BF16)
        g_row, b_row = ln_g[i][None, :], ln_b[i][None, :]
        g_col, b_col = ln_g[i][:, None], ln_b[i][:, None]
        if i % 2 == 0:
            wb = w_in_conv[j].astype(BF16)
            xp, st_p = _conv_layer_prompt(xp, wb, conv_w[j], mkT_all[i], mvT_all[i],
                                          woutb, g_row, b_row, alpha=alpha, tm=tm)
            cv_p.append(st_p)
            st = state_conv[j]
            tokT, uT, qmT, sgT = _s_pre_conv(xsT, wb, st[:, 0].T, st[:, 1].T, conv_w[j].T)
            cv_s.append(jnp.stack([st[:, 1], uT.T], axis=1))
            crsT = _s_cross(qmT, mk5, mv5, i, bc=bc)
        else:
            wt = w_in_fox[j].T
            o_f, o_qm, o_g = 3 * tw, 3 * tw + nh, 3 * tw + nh + X_WIDTH
            wT = jnp.concatenate([wt[0:o_f], wt[o_qm:], wt[o_f:o_qm],
                                  jnp.zeros((hp - nh, d_model), F32)], axis=0).astype(BF16)
            bf = jnp.pad(b_forget[j], (0, hp - nh))[:, None]
            q, kT_all, vT_all, kTb, vTb, lfT, d, dTb, jlo, qm, sg = _fox_proj_prompt(
                xp, wT, bf, kv_all, layer=j, n_layers=depth // 2, tm=tm, tb=tb, nh=nh)
            kv_all = (kT_all, vT_all)
            mt = _fox_attn_prompt(jlo, q, kTb, vTb, d, dTb, sg)
            xp = _fox_finish_prompt(xp, mt, qm, sg, mkT_all[i], mvT_all[i], woutb, g_row, b_row,
                                    alpha=alpha, tm=tm)
            fl_p.append(jnp.transpose(lfT[:, :nh, :], (0, 2, 1)))

            qT, knT, vnT, lfnT, slT, qmT, sgT = _s_pre_fox(xsT, wT, bf, nh=nh)
            tokT = _s_fox(page_table, qT, vnT, slT, lfnT, lf5, k5, v5, j, nh=nh)
            crsT = _s_cross(qmT, mk5, mv5, i, bc=bc)
            fk_s.append(knT.T.reshape(nb, 1, nh, HEAD_DIM))
            fv_s.append(vnT.T.reshape(nb, 1, nh, HEAD_DIM))
            fl_s.append(lfnT[:nh].T.reshape(nb, 1, nh))
        xsT = _s_post(xsT, tokT, crsT, sgT, woutb, g_col, b_col, alpha=alpha)

    n_mem = mem_prompt.shape[1]
    mk_out = jnp.transpose(mkT_all.reshape(depth, bsz, X_HEADS, HEAD_DIM, n_mem), (0, 1, 4, 2, 3))
    mv_out = jnp.transpose(mvT_all.reshape(depth, bsz, X_HEADS, HEAD_DIM, n_mem), (0, 1, 4, 2, 3))
    n_fox = depth // 2
    fk_all = jnp.transpose(kv_all[0].reshape(n_fox, bsz, nh, HEAD_DIM, seq), (0, 1, 4, 2, 3))
    fv_all = jnp.transpose(kv_all[1].reshape(n_fox, bsz, nh, HEAD_DIM, seq), (0, 1, 4, 2, 3))
    return (xp, xsT.T.reshape(nb, 1, d_model),
            fk_all, fv_all, jnp.stack(fl_p),
            jnp.stack(fk_s), jnp.stack(fv_s), jnp.stack(fl_s),
            jnp.stack(cv_p), jnp.stack(cv_s), mk_out, mv_out)
```

```python
import functools

import jax
import jax.numpy as jnp
from jax import lax
from jax.experimental import pallas as pl
from jax.experimental.pallas import tpu as pltpu

F32 = jnp.float32
BF16 = jnp.bfloat16

HEAD_DIM = 64
X_HEADS = 4
X_WIDTH = X_HEADS * HEAD_DIM
CONV_WIDTH = 3
LN_EPS = 1e-5
QK_SCALE = HEAD_DIM ** -0.5
LOG2E = 1.4426950408889634
SKIP_LOG2 = 150.0
NORM_SLACK = 1.02
MAX_HEAD_GROUP = 6
LANES = 128
VMEM_LIMIT = 56 * 1024 * 1024

_NT = (((1,), (1,)), ((), ()))


def _dot(a, b):
    return jnp.dot(a, b, preferred_element_type=F32)


def _dot_nt(a, b):
    return lax.dot_general(a, b, _NT, preferred_element_type=F32)


def _silu(g):
    return g * jax.nn.sigmoid(g)


def _log_sigmoid(x):
    return jnp.minimum(x, 0.0) - jnp.log1p(jnp.exp(-jnp.abs(x)))


def _layer_norm(y, g, b, axis):
    mu = jnp.mean(y, axis=axis, keepdims=True)
    yc = y - mu
    var = jnp.mean(yc * yc, axis=axis, keepdims=True)
    return yc * lax.rsqrt(var + LN_EPS) * g + b


def _split3(x):
    hi = x.astype(BF16)
    r1 = x - hi.astype(F32)
    mid = r1.astype(BF16)
    lo = (r1 - mid.astype(F32)).astype(BF16)
    return hi, mid, lo


def _const_spec(shape):
    nd = len(shape)
    return pl.BlockSpec(shape, lambda *_: (0,) * nd, pipeline_mode=pl.Buffered(1))


def _mem_kv_kernel(mem_ref, wT_ref, kT_ref, vT_ref):
    memb = mem_ref[0].astype(BF16)
    kvT = _dot_nt(wT_ref[0], memb)
    kT_ref[0, 0] = kvT[:X_WIDTH]
    vT_ref[0, 0] = kvT[X_WIDTH:]


def _mem_kv(mem_prompt, w_mem_kvT):
    depth = w_mem_kvT.shape[0]
    bsz, n_mem, d_model = mem_prompt.shape
    out = jax.ShapeDtypeStruct((depth, bsz, X_WIDTH, n_mem), F32)
    return pl.pallas_call(
        _mem_kv_kernel,
        grid=(depth, bsz),
        in_specs=[pl.BlockSpec((1, n_mem, d_model), lambda i, b: (b, 0, 0)),
                  pl.BlockSpec((1, 2 * X_WIDTH, d_model), lambda i, b: (i, 0, 0))],
        out_specs=[pl.BlockSpec((1, 1, X_WIDTH, n_mem), lambda i, b: (i, b, 0, 0))] * 2,
        out_shape=[out, out],
        compiler_params=pltpu.CompilerParams(dimension_semantics=("arbitrary", "arbitrary")),
        name="mem_kv",
    )(mem_prompt, w_mem_kvT)


def _cross_prompt(qm, kT, vT):
    lane = lax.broadcasted_iota(jnp.int32, qm.shape, 1)
    vrow = lax.broadcasted_iota(jnp.int32, vT.shape, 0)
    out = None
    for h in range(X_HEADS):
        lo, hi = h * HEAD_DIM, (h + 1) * HEAD_DIM
        qh = jnp.where((lane >= lo) & (lane < hi), qm, 0.0).astype(BF16)
        s = _dot(qh, kT)
        m = jnp.max(s, axis=-1, keepdims=True)
        e = jnp.exp(s - m)
        p = e * (1.0 / jnp.sum(e, axis=-1, keepdims=True))
        vh = jnp.where((vrow >= lo) & (vrow < hi), vT, jnp.zeros_like(vT))
        o = _dot_nt(p.astype(BF16), vh)
        out = o if out is None else out + o
    return out


def _finish_prompt(x, tok, crs, sg, wout_ref, g, b, alpha):
    tw = tok.shape[1]
    mt = (tok * sg[:, :tw]).astype(BF16)
    mc = (crs * sg[:, tw:]).astype(BF16)
    y = alpha * x + _dot(mt, wout_ref[0:tw, :]) + _dot(mc, wout_ref[tw:, :])
    return _layer_norm(y, g, b, axis=-1)


def _conv_layer_kernel(x_ref, win_ref, cw_ref, mkT_ref, mvT_ref, wout_ref, g_ref, b_ref,
                       xo_ref, st_ref, ubuf, *, alpha, tw):
    t = pl.program_id(1)
    tm = x_ref.shape[1]

    @pl.when(t == 0)
    def _():
        ubuf[0:8, :] = jnp.zeros((8, tw), F32)

    x = x_ref[0]
    xb = x.astype(BF16)
    bg = _dot(xb, win_ref[:, 0:tw])
    u = _dot(xb, win_ref[:, tw:2 * tw]) * _dot(xb, win_ref[:, 2 * tw:3 * tw])
    ubuf[8:8 + tm, :] = u
    cw = cw_ref[...]
    y = cw[0:1] * ubuf[6:6 + tm, :] + cw[1:2] * ubuf[7:7 + tm, :] + cw[2:3] * u
    tok = bg * y
    last = ubuf[tm + 6:tm + 8, :]
    ubuf[6:8, :] = last
    st_ref[0] = last

    qm = _dot(xb, win_ref[:, 3 * tw:3 * tw + X_WIDTH]) * QK_SCALE
    gate = _dot(xb, win_ref[:, 3 * tw + X_WIDTH:])
    crs = _cross_prompt(qm, mkT_ref[0].astype(BF16), mvT_ref[0].astype(BF16))
    xo_ref[0] = _finish_prompt(x, tok, crs, _silu(gate), wout_ref, g_ref[...], b_ref[...], alpha)


def _conv_layer_prompt(x, win, cw, mkT, mvT, wout, g, b, *, alpha, tm):
    bsz, seq, d_model = x.shape
    tw = cw.shape[1]
    n_mem = mkT.shape[2]
    kern = functools.partial(_conv_layer_kernel, alpha=alpha, tw=tw)
    return pl.pallas_call(
        kern,
        grid=(bsz, seq // tm),
        in_specs=[pl.BlockSpec((1, tm, d_model), lambda bb, t: (bb, t, 0)),
                  _const_spec(win.shape),
                  _const_spec(cw.shape),
                  pl.BlockSpec((1, X_WIDTH, n_mem), lambda bb, t: (bb, 0, 0)),
                  pl.BlockSpec((1, X_WIDTH, n_mem), lambda bb, t: (bb, 0, 0)),
                  _const_spec(wout.shape),
                  _const_spec(g.shape),
                  _const_spec(b.shape)],
        out_specs=[pl.BlockSpec((1, tm, d_model), lambda bb, t: (bb, t, 0)),
                   pl.BlockSpec((1, CONV_WIDTH - 1, tw), lambda bb, t: (bb, 0, 0))],
        out_shape=[jax.ShapeDtypeStruct(x.shape, F32),
                   jax.ShapeDtypeStruct((bsz, CONV_WIDTH - 1, tw), F32)],
        scratch_shapes=[pltpu.VMEM((tm + 8, tw), F32)],
        compiler_params=pltpu.CompilerParams(dimension_semantics=("arbitrary", "arbitrary"),
                                             vmem_limit_bytes=VMEM_LIMIT),
        name="conv_layer_prompt",
    )(x, win, cw, mkT, mvT, wout, g, b)


def _fox_proj_kernel(x_ref, wqkv_ref, wmg_ref, wf_ref, bf_ref, *rest, tw, nh, tb, n_alias):
    (q_ref, kT_ref, vT_ref, kTb_ref, vTb_ref, lfT_ref, d_ref, dTb_ref, jlo_ref, qm_ref, sg_ref,
     carry_ref, tri_ref, hk_ref, hd_ref) = rest[n_alias:]
    t = pl.program_id(1)
    tm = x_ref.shape[1]
    hp = bf_ref.shape[0]
    nsub = tm // tb

    @pl.when(t == 0)
    def _():
        carry_ref[...] = jnp.zeros(carry_ref.shape, F32)
        hk_ref[...] = jnp.zeros(hk_ref.shape, F32)
        hd_ref[...] = jnp.zeros(hd_ref.shape, F32)
        r = lax.broadcasted_iota(jnp.int32, (tm, tm), 0)
        c = lax.broadcasted_iota(jnp.int32, (tm, tm), 1)
        tri_ref[...] = jnp.where(r <= c, 1.0, 0.0).astype(BF16)

    xb = x_ref[0].astype(BF16)
    q2 = (_dot_nt(xb, wqkv_ref[0:tw, :]) * (QK_SCALE * LOG2E)).astype(BF16)
    q_ref[0] = q2
    qm_ref[0] = (_dot_nt(xb, wmg_ref[0:X_WIDTH, :]) * QK_SCALE).astype(BF16)
    sg_ref[0] = _silu(_dot_nt(xb, wmg_ref[X_WIDTH:, :])).astype(BF16)

    kT = _dot_nt(wqkv_ref[tw:2 * tw, :], xb)
    vT = _dot_nt(wqkv_ref[2 * tw:3 * tw, :], xb)
    kT_ref[0] = kT
    vT_ref[0] = vT
    kTb = kT.astype(BF16)
    vTb = vT.astype(BF16)
    for u in range(nsub):
        for h in range(nh):
            kTb_ref[0, u, h] = kTb[h * HEAD_DIM:(h + 1) * HEAD_DIM, u * tb:(u + 1) * tb]
            vTb_ref[0, u, h] = vTb[h * HEAD_DIM:(h + 1) * HEAD_DIM, u * tb:(u + 1) * tb]

    fT = _dot_nt(wf_ref[...], xb) + bf_ref[...]
    hrow = lax.broadcasted_iota(jnp.int32, (hp, tm), 0)
    lf = jnp.where(hrow < nh, _log_sigmoid(fT), 0.0)
    lfT_ref[0] = lf

    cs3 = _dot(jnp.concatenate(_split3(lf), axis=0), tri_ref[...])
    cs = cs3[0:hp] + cs3[hp:2 * hp] + cs3[2 * hp:3 * hp] + carry_ref[:, 0:1]
    carry_ref[...] = jnp.broadcast_to(cs[:, tm - 1:tm], carry_ref.shape)
    cs2 = cs * LOG2E
    for u in range(nsub):
        for h in range(hp):
            dTb_ref[0, u, h] = cs2[h:h + 1, u * tb:(u + 1) * tb]
    d3 = [t.astype(F32) for t in _split3(cs2)] + [jnp.zeros((LANES - 3 * hp, tm), F32)]
    d_ref[0] = jnp.concatenate(d3, axis=0).T.astype(BF16)

    head_of_col = lax.broadcasted_iota(jnp.int32, (LANES, tw), 1) // HEAD_DIM
    sel = jnp.where(head_of_col == lax.broadcasted_iota(jnp.int32, (LANES, tw), 0), 1.0, 0.0).astype(BF16)
    q2f = q2.astype(F32)
    nq = _dot_nt(sel, (q2f * q2f).astype(BF16))
    kf = kTb.astype(F32)
    ksq = kf * kf
    nk = jnp.concatenate([jnp.sum(ksq[h * HEAD_DIM:(h + 1) * HEAD_DIM], axis=0, keepdims=True)
                          for h in range(nh)] + [jnp.zeros((hp - nh, tm), F32)], axis=0)
    lane = lax.broadcasted_iota(jnp.int32, (hp, LANES), 1)
    lane_f = lane.astype(F32)
    for u in range(nsub):
        i = t * nsub + u
        qmax = jnp.sqrt(jnp.max(nq[0:hp, u * tb:(u + 1) * tb], axis=1, keepdims=True))
        kmax = jnp.sqrt(jnp.max(nk[:, u * tb:(u + 1) * tb], axis=1, keepdims=True))
        hk = jnp.where(lane == i, kmax, hk_ref[...])
        hd = jnp.where(lane == i, cs2[:, (u + 1) * tb - 1:(u + 1) * tb], hd_ref[...])
        hk_ref[...] = hk
        hd_ref[...] = hd
        gap = qmax * (hk + kmax) * NORM_SLACK + cs2[:, u * tb:u * tb + 1] - hd
        alive = (gap >= -SKIP_LOG2) & (lane < i)
        i_f = i.astype(F32)
        jlo_ref[0, u] = jnp.min(jnp.where(alive, lane_f, i_f), axis=1, keepdims=True).astype(jnp.int32)


def _fox_proj_prompt(x, wqkv, wmg, wf, bf, kv_prev, *, layer, n_layers, tm, tb, nh):
    bsz, seq, d_model = x.shape
    tw = nh * HEAD_DIM
    hp = bf.shape[0]
    nblk = seq // tb
    nsub = tm // tb
    assert nblk <= LANES
    aliased = [] if kv_prev is None else list(kv_prev)
    kern = functools.partial(_fox_proj_kernel, tw=tw, nh=nh, tb=tb, n_alias=len(aliased))
    row = lambda bb, t: (bb, t, 0)
    colT = lambda bb, t: (bb, 0, t)
    blocked = lambda bb, t: (bb, t, 0, 0, 0)
    kv_all = jax.ShapeDtypeStruct((n_layers, bsz, tw, seq), F32)
    kv_spec = pl.BlockSpec((None, 1, tw, tm), lambda bb, t: (layer, bb, 0, t))
    out_shape = [
        jax.ShapeDtypeStruct((bsz, seq, tw), BF16),
        kv_all,
        kv_all,
        jax.ShapeDtypeStruct((bsz, nblk, nh, HEAD_DIM, tb), BF16),
        jax.ShapeDtypeStruct((bsz, nblk, nh, HEAD_DIM, tb), BF16),
        jax.ShapeDtypeStruct((bsz, hp, seq), F32),
        jax.ShapeDtypeStruct((bsz, seq, LANES), BF16),
        jax.ShapeDtypeStruct((bsz, nblk, hp, 1, tb), F32),
        jax.ShapeDtypeStruct((bsz, nblk, hp, 1), jnp.int32),
        jax.ShapeDtypeStruct((bsz, seq, X_WIDTH), BF16),
        jax.ShapeDtypeStruct((bsz, seq, d_model), BF16),
    ]
    out_specs = [
        pl.BlockSpec((1, tm, tw), row),
        kv_spec,
        kv_spec,
        pl.BlockSpec((1, nsub, nh, HEAD_DIM, tb), blocked),
        pl.BlockSpec((1, nsub, nh, HEAD_DIM, tb), blocked),
        pl.BlockSpec((1, hp, tm), colT),
        pl.BlockSpec((1, tm, LANES), row),
        pl.BlockSpec((1, nsub, hp, 1, tb), blocked),
        pl.BlockSpec((1, nsub, hp, 1), lambda bb, t: (bb, t, 0, 0)),
        pl.BlockSpec((1, tm, X_WIDTH), row),
        pl.BlockSpec((1, tm, d_model), row),
    ]
    n_in = 5
    return pl.pallas_call(
        kern,
        grid=(bsz, seq // tm),
        in_specs=[pl.BlockSpec((1, tm, d_model), row), _const_spec(wqkv.shape), _const_spec(wmg.shape),
                  _const_spec(wf.shape),
                  _const_spec(bf.shape)] + [pl.BlockSpec(memory_space=pl.ANY)] * len(aliased),
        out_specs=out_specs,
        out_shape=out_shape,
        input_output_aliases={n_in + k: 1 + k for k in range(len(aliased))},
        scratch_shapes=[pltpu.VMEM((hp, LANES), F32), pltpu.VMEM((tm, tm), BF16),
                        pltpu.VMEM((hp, LANES), F32), pltpu.VMEM((hp, LANES), F32)],
        compiler_params=pltpu.CompilerParams(dimension_semantics=("arbitrary", "arbitrary"),
                                             vmem_limit_bytes=VMEM_LIMIT),
        name="fox_proj_prompt",
    )(x, wqkv, wmg, wf, bf, *aliased)


def _fox_attn_kernel(jlo_ref, q_ref, kTb_ref, vTb_ref, d_ref, dTb_ref, sg_ref, mt_ref,
                     qh_scr, dq_scr, m_scr, l_scr, acc_scr, tok_scr, *, nh, hp, group):
    bb = pl.program_id(0)
    i = pl.program_id(1)
    nblk = pl.num_programs(1)
    tq = q_ref.shape[1]
    tk = kTb_ref.shape[4]
    reps = tk // LANES

    d3 = d_ref[0]
    term = lax.broadcasted_iota(jnp.int32, (LANES, LANES), 0)
    for h in range(nh):
        qh_scr[h] = q_ref[0, :, h * HEAD_DIM:(h + 1) * HEAD_DIM]
        pick = (term == h) | (term == hp + h) | (term == 2 * hp + h)
        dq_scr[h] = _dot(d3, jnp.where(pick, 1.0, 0.0).astype(BF16))

    row = lax.broadcasted_iota(jnp.int32, (tq, tk), 0)
    col = lax.broadcasted_iota(jnp.int32, (tq, tk), 1)
    causal = col <= row

    def group_body(g, carry):
        heads = [g * group + k for k in range(group)]
        m_scr[...] = jnp.full(m_scr.shape, -jnp.inf, F32)
        l_scr[...] = jnp.zeros(l_scr.shape, F32)
        acc_scr[...] = jnp.zeros(acc_scr.shape, F32)
        qs = [qh_scr[h] for h in heads]
        dqts = [jnp.concatenate([dq_scr[h]] * reps, axis=1) for h in heads]

        def step(j, masked):
            for k, h in enumerate(heads):
                s = _dot(qs[k], kTb_ref[0, j, h])
                s = s + dqts[k] - dTb_ref[0, j, h]
                if masked:
                    s = jnp.where(causal, s, -jnp.inf)
                m_old = m_scr[k]
                m_new = jnp.maximum(m_old, jnp.max(s, axis=1, keepdims=True))
                a = jnp.exp2(m_old - m_new)
                p = jnp.exp2(s - jnp.concatenate([m_new] * reps, axis=1))
                l_scr[k] = a * l_scr[k] + jnp.sum(p, axis=1, keepdims=True)
                m_scr[k] = m_new
                pv = _dot_nt(p.astype(BF16), vTb_ref[0, j, h])
                acc_scr[k] = a[:, 0:HEAD_DIM] * acc_scr[k] + pv

        def loop_body(j, c):
            step(j, False)
            return c

        base = (bb * nblk + i) * hp
        j0 = jlo_ref[base + heads[0]]
        for h in heads[1:]:
            j0 = jnp.minimum(j0, jlo_ref[base + h])
        lax.fori_loop(j0, i, loop_body, 0)
        step(i, True)
        for k, h in enumerate(heads):
            tok_scr[h] = acc_scr[k] / l_scr[k][:, 0:HEAD_DIM]
        return carry

    lax.fori_loop(0, nh // group, group_body, 0)

    tok = jnp.concatenate([tok_scr[h] for h in range(nh)], axis=1)
    mt_ref[0] = (tok * sg_ref[0][:, 0:nh * HEAD_DIM].astype(F32)).astype(BF16)


def _fox_finish_kernel(x_ref, mt_ref, qm_ref, sg_ref, mkT_ref, mvT_ref, wout_ref, g_ref, b_ref, xo_ref,
                       *, alpha):
    tw = mt_ref.shape[2]
    crs = _cross_prompt(qm_ref[0].astype(F32), mkT_ref[0].astype(BF16), mvT_ref[0].astype(BF16))
    mc = (crs * sg_ref[0][:, tw:].astype(F32)).astype(BF16)
    y = alpha * x_ref[0] + _dot(mt_ref[0], wout_ref[0:tw, :]) + _dot(mc, wout_ref[tw:, :])
    xo_ref[0] = _layer_norm(y, g_ref[...], b_ref[...], axis=-1)


def _fox_finish_prompt(x, mt, qm, sg, mkT, mvT, wout, g, b, *, alpha, tm):
    bsz, seq, d_model = x.shape
    n_mem = mkT.shape[2]
    row = lambda w: pl.BlockSpec((1, tm, w), lambda bb, t: (bb, t, 0))
    per_b = pl.BlockSpec((1, X_WIDTH, n_mem), lambda bb, t: (bb, 0, 0))
    return pl.pallas_call(
        functools.partial(_fox_finish_kernel, alpha=alpha),
        grid=(bsz, seq // tm),
        in_specs=[row(d_model), row(mt.shape[2]), row(X_WIDTH), row(d_model), per_b, per_b,
                  _const_spec(wout.shape), _const_spec(g.shape), _const_spec(b.shape)],
        out_specs=row(d_model),
        out_shape=jax.ShapeDtypeStruct(x.shape, F32),
        compiler_params=pltpu.CompilerParams(dimension_semantics=("arbitrary", "arbitrary"),
                                             vmem_limit_bytes=VMEM_LIMIT),
        name="fox_finish_prompt",
    )(x, mt, qm, sg, mkT, mvT, wout, g, b)


def _fox_attn_prompt(jlo, q, kTb, vTb, d, dTb, sg):
    bsz, seq, d_model = sg.shape
    _, nblk, nh, _, tk = kTb.shape
    tq = tk
    tw = nh * HEAD_DIM
    hp = dTb.shape[2]
    group = max(k for k in range(1, MAX_HEAD_GROUP + 1) if nh % k == 0)
    kern = functools.partial(_fox_attn_kernel, nh=nh, hp=hp, group=group)
    row = lambda bb, t, jl: (bb, t, 0)
    per_b5 = lambda bb, t, jl: (bb, 0, 0, 0, 0)

    grid_spec = pltpu.PrefetchScalarGridSpec(
        num_scalar_prefetch=1,
        grid=(bsz, nblk),
        in_specs=[pl.BlockSpec((1, tq, tw), row),
                  pl.BlockSpec((1, nblk, nh, HEAD_DIM, tk), per_b5, pipeline_mode=pl.Buffered(1)),
                  pl.BlockSpec((1, nblk, nh, HEAD_DIM, tk), per_b5, pipeline_mode=pl.Buffered(1)),
                  pl.BlockSpec((1, tq, LANES), row),
                  pl.BlockSpec((1, nblk, hp, 1, tk), per_b5, pipeline_mode=pl.Buffered(1)),
                  pl.BlockSpec((1, tq, d_model), row)],
        out_specs=pl.BlockSpec((1, tq, tw), row),
        scratch_shapes=[pltpu.VMEM((nh, tq, HEAD_DIM), BF16),
                        pltpu.VMEM((nh, tq, LANES), F32),
                        pltpu.VMEM((group, tq, LANES), F32),
                        pltpu.VMEM((group, tq, LANES), F32),
                        pltpu.VMEM((group, tq, HEAD_DIM), F32),
                        pltpu.VMEM((nh, tq, HEAD_DIM), F32)],
    )
    return pl.pallas_call(
        kern,
        grid_spec=grid_spec,
        out_shape=jax.ShapeDtypeStruct((bsz, seq, tw), BF16),
        compiler_params=pltpu.CompilerParams(dimension_semantics=("arbitrary", "arbitrary"),
                                             vmem_limit_bytes=VMEM_LIMIT),
        name="fox_attn_prompt",
    )(jlo.reshape(-1), q, kTb, vTb, d, dTb, sg)


def _s_pre_conv_kernel(xT_ref, w_ref, st0_ref, st1_ref, cw_ref,
                       tokT_ref, uT_ref, qmT_ref, sgT_ref, *, tw):
    xb = xT_ref[...].T.astype(BF16)
    proj = lambda lo, hi: _dot(xb, w_ref[:, lo:hi]).T
    bg = proj(0, tw)
    u = proj(tw, 2 * tw) * proj(2 * tw, 3 * tw)
    cw = cw_ref[...]
    y = cw[:, 0:1] * st0_ref[...] + cw[:, 1:2] * st1_ref[...] + cw[:, 2:3] * u
    tokT_ref[...] = bg * y
    uT_ref[...] = u
    qmT_ref[...] = proj(3 * tw, 3 * tw + X_WIDTH) * QK_SCALE
    sgT_ref[...] = _silu(proj(3 * tw + X_WIDTH, w_ref.shape[1]))


def _s_pre_conv(xT, wT, st0T, st1T, cwT):
    d_model, nb = xT.shape
    tw = cwT.shape[0]
    f = lambda r: jax.ShapeDtypeStruct((r, nb), F32)
    return pl.pallas_call(
        functools.partial(_s_pre_conv_kernel, tw=tw),
        out_shape=[f(tw), f(tw), f(X_WIDTH), f(d_model)],
        compiler_params=pltpu.CompilerParams(vmem_limit_bytes=VMEM_LIMIT),
        name="sample_pre_conv",
    )(xT, wT, st0T, st1T, cwT)


def _s_pre_fox_kernel(xT_ref, wqkv_ref, wmg_ref, wf_ref, bf_ref,
                      qT_ref, kT_ref, vT_ref, lfT_ref, slT_ref, qmT_ref, sgT_ref, *, tw, nh):
    xb = xT_ref[...].astype(BF16)
    nb = xb.shape[1]
    hp = bf_ref.shape[0]
    q = _dot(wqkv_ref[0:tw, :], xb) * QK_SCALE
    k = _dot(wqkv_ref[tw:2 * tw, :], xb)
    qT_ref[...] = q
    kT_ref[...] = k
    vT_ref[...] = _dot(wqkv_ref[2 * tw:3 * tw, :], xb)
    qmT_ref[...] = _dot(wmg_ref[0:X_WIDTH, :], xb) * QK_SCALE
    sgT_ref[...] = _silu(_dot(wmg_ref[X_WIDTH:, :], xb))
    fT = _dot(wf_ref[...], xb) + bf_ref[...]
    hrow = lax.broadcasted_iota(jnp.int32, (hp, nb), 0)
    lfT_ref[...] = jnp.where(hrow < nh, _log_sigmoid(fT), 0.0)
    qk = q * k
    rows = [jnp.sum(qk[h * HEAD_DIM:(h + 1) * HEAD_DIM], axis=0, keepdims=True) for h in range(nh)]
    rows.append(jnp.zeros((hp - nh, nb), F32))
    slT_ref[...] = jnp.concatenate(rows, axis=0)


def _s_pre_fox(xT, wqkv, wmg, wf, bf, *, nh):
    d_model, nb = xT.shape
    tw = nh * HEAD_DIM
    hp = bf.shape[0]
    f = lambda r: jax.ShapeDtypeStruct((r, nb), F32)
    return pl.pallas_call(
        functools.partial(_s_pre_fox_kernel, tw=tw, nh=nh),
        out_shape=[f(tw), f(tw), f(tw), f(hp), f(hp), f(X_WIDTH), f(d_model)],
        compiler_params=pltpu.CompilerParams(vmem_limit_bytes=VMEM_LIMIT),
        name="sample_pre_fox",
    )(xT, wqkv, wmg, wf, bf)


def _column(ref_or_val, onehot):
    return jnp.sum(jnp.where(onehot, ref_or_val, 0.0), axis=1, keepdims=True)


def _cross_sample_cols(qcs, mks, mvs):
    pairs = [(i, h) for i in range(len(qcs)) for h in range(X_HEADS)]
    hd = lambda h: slice(h * HEAD_DIM, (h + 1) * HEAD_DIM)
    s = [jnp.sum(qcs[i][hd(h)] * mks[i][h], axis=0, keepdims=True) for i, h in pairs]
    m = [jnp.max(x, axis=1, keepdims=True) for x in s]
    e = [jnp.exp(x - y) for x, y in zip(s, m)]
    r = [1.0 / jnp.sum(x, axis=1, keepdims=True) for x in e]
    p = [x * y for x, y in zip(e, r)]
    o = [jnp.sum(x * mvs[i][h], axis=1, keepdims=True) for x, (i, h) in zip(p, pairs)]
    return [jnp.concatenate(o[i * X_HEADS:(i + 1) * X_HEADS], axis=0) for i in range(len(qcs))]


def _s_cross_kernel(qmT_ref, mk_ref, mv_ref, crsT_ref, *, bc):
    c = pl.program_id(0)

    @pl.when(c == 0)
    def _():
        crsT_ref[...] = jnp.zeros(crsT_ref.shape, F32)

    lane = lax.broadcasted_iota(jnp.int32, crsT_ref.shape, 1)
    qmT = qmT_ref[...]
    hots = [lane == c * bc + bb for bb in range(bc)]
    cols = _cross_sample_cols([_column(qmT, hot) for hot in hots],
                              [mk_ref.at[bb] for bb in range(bc)], [mv_ref.at[bb] for bb in range(bc)])
    out = crsT_ref[...]
    for bb in range(bc):
        out = jnp.where(hots[bb], cols[bb], out)
    crsT_ref[...] = out


def _s_cross(qmT, mk5, mv5, layer, *, bc):
    nb = qmT.shape[1]
    n_mem = mk5.shape[4]
    blk = pl.BlockSpec((None, bc, X_HEADS, HEAD_DIM, n_mem), lambda c: (layer, c, 0, 0, 0))
    return pl.pallas_call(
        functools.partial(_s_cross_kernel, bc=bc),
        grid=(nb // bc,),
        in_specs=[pl.BlockSpec(qmT.shape, lambda c: (0, 0)), blk, blk],
        out_specs=pl.BlockSpec(qmT.shape, lambda c: (0, 0)),
        out_shape=jax.ShapeDtypeStruct(qmT.shape, F32),
        compiler_params=pltpu.CompilerParams(dimension_semantics=("arbitrary",),
                                             vmem_limit_bytes=VMEM_LIMIT),
        name="sample_cross",
    )(qmT, mk5, mv5)


def _s_fox_scores_kernel(pt_ref, qT_ref, slT_ref, lfnT_ref, *rest, nh, n_pages, rows):
    n = rows * n_pages
    p_ref, pself_ref, l_ref, alive_ref, l_scr, s_scr = rest[2 * n:]
    for r in range(rows):
        _s_fox_scores_row(pl.program_id(0) * rows + r, qT_ref, slT_ref, lfnT_ref,
                          rest[r * n_pages:(r + 1) * n_pages], rest[n + r * n_pages:n + (r + 1) * n_pages],
                          p_ref.at[r], pself_ref.at[r], l_ref.at[r], alive_ref.at[r],
                          l_scr.at[r], s_scr.at[r], nh=nh, n_pages=n_pages)


def _s_fox_scores_row(b, qT_ref, slT_ref, lfnT_ref, lf_pages, k_pages,
                      p_ref, pself_ref, l_ref, alive_ref, l_scr, s_scr, *, nh, n_pages):
    page = k_pages[0].shape[2]
    nrow = nh * n_pages

    def onehot(shape):
        return lax.broadcasted_iota(jnp.int32, shape, 1) == b

    qc = _column(qT_ref[...], onehot(qT_ref.shape))
    slc = _column(slT_ref[...], onehot(slT_ref.shape))
    lfn = _column(lfnT_ref[...], onehot(lfnT_ref.shape))

    for p in range(n_pages):
        for h in range(nh):
            l_scr[h * n_pages + p:h * n_pages + p + 1, :] = lf_pages[p][h]
    lfp = l_scr[...]
    r = lax.broadcasted_iota(jnp.int32, (page, page), 0)
    c = lax.broadcasted_iota(jnp.int32, (page, page), 1)
    after_in_page = jnp.where(r > c, 1.0, 0.0).astype(BF16)
    w3 = _dot(jnp.concatenate(_split3(lfp), axis=0), after_in_page)
    within = w3[0:nrow] + w3[nrow:2 * nrow] + w3[2 * nrow:3 * nrow]
    tot = jnp.broadcast_to(jnp.sum(lfp, axis=1, keepdims=True), (nrow, page))
    ru = lax.broadcasted_iota(jnp.int32, (nrow, nrow), 0)
    cu = lax.broadcasted_iota(jnp.int32, (nrow, nrow), 1)
    later_page = jnp.where((cu > ru) & (cu // n_pages == ru // n_pages), 1.0, 0.0).astype(BF16)
    a3 = _dot(later_page, jnp.concatenate(_split3(tot), axis=1))
    bias = within + a3[:, 0:page] + a3[:, page:2 * page] + a3[:, 2 * page:3 * page]

    for h in range(nh):
        qh = qc[h * HEAD_DIM:(h + 1) * HEAD_DIM]
        for p in range(n_pages):
            s_scr[h * n_pages + p:h * n_pages + p + 1, :] = jnp.sum(qh * k_pages[p][h], axis=0, keepdims=True)
    s_all = s_scr[...] + bias

    hs = range(nh)
    s_h = [s_all[h * n_pages:(h + 1) * n_pages] + lfn[h:h + 1] for h in hs]
    sl = [slc[h:h + 1] for h in hs]
    m = [jnp.maximum(jnp.max(jnp.max(s_h[h], axis=1, keepdims=True), axis=0, keepdims=True), sl[h])
         for h in hs]
    p_h = [jnp.exp(s_h[h] - m[h]) for h in hs]
    p_self = [jnp.exp(sl[h] - m[h]) for h in hs]
    l = [jnp.sum(jnp.sum(p_h[h], axis=1, keepdims=True), axis=0, keepdims=True) + p_self[h] for h in hs]
    pm = [jnp.max(p_h[h], axis=1, keepdims=True) for h in hs]
    page_max = pm[0]
    for h in hs:
        p_ref[h * n_pages:(h + 1) * n_pages, :] = p_h[h]
        pself_ref[h:h + 1, :] = jnp.broadcast_to(p_self[h], (1, LANES))
        l_ref[h:h + 1, :] = jnp.broadcast_to(l[h], (1, LANES))
        page_max = jnp.maximum(page_max, pm[h])
    hp = pself_ref.shape[0]
    if hp > nh:
        pself_ref[nh:hp, :] = jnp.zeros((hp - nh, LANES), F32)
        l_ref[nh:hp, :] = jnp.ones((hp - nh, LANES), F32)
    alive_ref[...] = jnp.broadcast_to(page_max, (n_pages, LANES))


def _s_fox_values_kernel(pt_ref, p_ref, pself_ref, l_ref, vnT_ref, *rest, nh, n_pages, rows):
    v_pages = rest[0:rows * n_pages]
    tokT_ref = rest[rows * n_pages]
    c = pl.program_id(0)

    @pl.when(c == 0)
    def _():
        tokT_ref[...] = jnp.zeros(tokT_ref.shape, F32)

    lane = lax.broadcasted_iota(jnp.int32, vnT_ref.shape, 1)
    lane_h = lax.broadcasted_iota(jnp.int32, (HEAD_DIM, tokT_ref.shape[1]), 1)
    for r in range(rows):
        b = c * rows + r
        vnc = _column(vnT_ref[...], lane == b)
        sums = []
        for h in range(nh):
            acc = p_ref[r, h * n_pages:h * n_pages + 1, :] * v_pages[r * n_pages][h]
            for p in range(1, n_pages):
                acc = acc + p_ref[r, h * n_pages + p:h * n_pages + p + 1, :] * v_pages[r * n_pages + p][h]
            sums.append(jnp.sum(acc, axis=1, keepdims=True))
        for h in range(nh):
            hd = slice(h * HEAD_DIM, (h + 1) * HEAD_DIM)
            o = (sums[h] + pself_ref[r, h:h + 1, 0:1] * vnc[hd]) / l_ref[r, h:h + 1, 0:1]
            tokT_ref[hd, :] = jnp.where(lane_h == b, o, tokT_ref[hd, :])


def _s_fox(page_table, qT, vnT, slT, lfnT, lf5, k5, v5, fox_layer, *, nh):
    tw, nb = qT.shape
    n_pages = page_table.shape[1]
    page = k5.shape[4]
    hp = slT.shape[0]
    nrow = nh * n_pages
    cparams = pltpu.CompilerParams(dimension_semantics=("arbitrary",), vmem_limit_bytes=VMEM_LIMIT)

    rows = 2 if nb % 2 == 0 else 1
    slots = [(r, p) for r in range(rows) for p in range(n_pages)]
    full = lambda a: pl.BlockSpec(a.shape, lambda c, pt: (0, 0))
    per_step = lambda n, w: pl.BlockSpec((rows, n, w), lambda c, pt: (c, 0, 0))
    kv_specs = [pl.BlockSpec((None, None, nh, HEAD_DIM, page),
                             lambda c, pt, r=r, p=p: (fox_layer, pt[c * rows + r, p], 0, 0, 0)) for r, p in slots]

    lf_specs = [pl.BlockSpec((None, nh, None, 1, page),
                             lambda c, pt, r=r, p=p: (fox_layer, 0, pt[c * rows + r, p], 0, 0)) for r, p in slots]
    probs, p_self, l_sum, page_max = pl.pallas_call(
        functools.partial(_s_fox_scores_kernel, nh=nh, n_pages=n_pages, rows=rows),
        grid_spec=pltpu.PrefetchScalarGridSpec(
            num_scalar_prefetch=1,
            grid=(nb // rows,),
            in_specs=[full(qT), full(slT), full(lfnT)] + lf_specs + kv_specs,
            out_specs=[per_step(nrow, page), per_step(hp, LANES), per_step(hp, LANES),
                       per_step(n_pages, LANES)],
            scratch_shapes=[pltpu.VMEM((rows, nrow, page), F32), pltpu.VMEM((rows, nrow, page), F32)]),
        out_shape=[jax.ShapeDtypeStruct((nb, nrow, page), F32),
                   jax.ShapeDtypeStruct((nb, hp, LANES), F32),
                   jax.ShapeDtypeStruct((nb, hp, LANES), F32),
                   jax.ShapeDtypeStruct((nb, n_pages, LANES), F32)],
        compiler_params=cparams,
        name="sample_fox_scores",
    )(page_table, qT, slT, lfnT, *([lf5] * len(slots)), *([k5] * len(slots)))

    alive = (page_max[:, :, 0] > 0.0).reshape(nb // rows, rows, n_pages)
    step_id = jnp.arange(nb // rows, dtype=jnp.int32)[:, None, None]
    last_live = jnp.maximum(lax.cummax(jnp.where(alive, step_id, -1), axis=0), 0)
    pt_eff = jnp.take_along_axis(page_table.reshape(nb // rows, rows, n_pages), last_live, axis=0)

    return pl.pallas_call(
        functools.partial(_s_fox_values_kernel, nh=nh, n_pages=n_pages, rows=rows),
        grid_spec=pltpu.PrefetchScalarGridSpec(
            num_scalar_prefetch=1,
            grid=(nb // rows,),
            in_specs=[per_step(nrow, page), per_step(hp, LANES), per_step(hp, LANES), full(vnT)] + kv_specs,
            out_specs=pl.BlockSpec((tw, nb), lambda c, pt: (0, 0))),
        out_shape=jax.ShapeDtypeStruct((tw, nb), F32),
        compiler_params=cparams,
        name="sample_fox_values",
    )(pt_eff.reshape(nb, n_pages), probs, p_self, l_sum, vnT, *([v5] * len(slots)))


def _s_post_kernel(xT_ref, tokT_ref, crsT_ref, sgT_ref, wo_ref, g_ref, b_ref, out_ref, *, alpha):
    tw = tokT_ref.shape[0]
    sg = sgT_ref[...]
    mt = (tokT_ref[...] * sg[0:tw]).T.astype(BF16)
    mc = (crsT_ref[...] * sg[tw:]).T.astype(BF16)
    y = alpha * xT_ref[...] + (_dot(mt, wo_ref[0:tw, :]) + _dot(mc, wo_ref[tw:, :])).T
    out_ref[...] = _layer_norm(y, g_ref[...], b_ref[...], axis=0)


def _s_post(xT, tokT, crsT, sgT, woT, gcol, bcol, *, alpha):
    return pl.pallas_call(
        functools.partial(_s_post_kernel, alpha=alpha),
        out_shape=jax.ShapeDtypeStruct(xT.shape, F32),
        compiler_params=pltpu.CompilerParams(vmem_limit_bytes=VMEM_LIMIT),
        name="sample_post",
    )(xT, tokT, crsT, sgT, woT, gcol, bcol)


def kernel(x_prompt, x_sample, mem_prompt, cache_fox_k, cache_fox_v, cache_fox_logf, state_conv,
           cache_mem_k, cache_mem_v, page_table, w_in_conv, conv_w, w_in_fox, b_forget, w_mem_kv,
           w_out, ln_g, ln_b):
    depth = w_out.shape[0]
    alpha = float((2 * depth) ** 0.25)
    bsz, seq, d_model = x_prompt.shape
    nb = x_sample.shape[0]
    tw = conv_w.shape[2]
    nh = tw // HEAD_DIM
    hp = -(-nh // 8) * 8
    tm = min(1024, seq)
    tb = min(256, seq)
    bc = min(8, nb)
    kv_all = None

    mkT_all, mvT_all = _mem_kv(mem_prompt, jnp.swapaxes(w_mem_kv, 1, 2).astype(BF16))
    mk5 = jnp.transpose(cache_mem_k, (0, 1, 3, 4, 2))
    mv5 = jnp.transpose(cache_mem_v, (0, 1, 3, 4, 2))
    k5 = jnp.transpose(cache_fox_k, (0, 1, 3, 4, 2))
    v5 = jnp.transpose(cache_fox_v, (0, 1, 3, 4, 2))
    lf5 = jnp.transpose(cache_fox_logf, (0, 3, 1, 2))[:, :, :, None, :]

    xp = x_prompt
    xsT = x_sample.reshape(nb, d_model).T
    fl_p, fk_s, fv_s, fl_s, cv_p, cv_s = [], [], [], [], [], []

    for i in range(depth):
        j = i // 2
        woutb = w_out[i].astype(BF16)
        g_row, b_row = ln_g[i][None, :], ln_b[i][None, :]
        g_col, b_col = ln_g[i][:, None], ln_b[i][:, None]
        if i % 2 == 0:
            wb = w_in_conv[j].astype(BF16)
            xp, st_p = _conv_layer_prompt(xp, wb, conv_w[j], mkT_all[i], mvT_all[i],
                                          woutb, g_row, b_row, alpha=alpha, tm=tm)
            cv_p.append(st_p)
            st = state_conv[j]
            tokT, uT, qmT, sgT = _s_pre_conv(xsT, wb, st[:, 0].T, st[:, 1].T, conv_w[j].T)
            cv_s.append(jnp.stack([st[:, 1], uT.T], axis=1))
            crsT = _s_cross(qmT, mk5, mv5, i, bc=bc)
        else:
            wt = w_in_fox[j].T.astype(BF16)
            o_f, o_qm = 3 * tw, 3 * tw + nh
            wqkv, wmg = wt[0:o_f], wt[o_qm:]
            wf = jnp.pad(wt[o_f:o_qm], ((0, hp - nh), (0, 0)))
            bf = jnp.pad(b_forget[j], (0, hp - nh))[:, None]
            q, kT_all, vT_all, kTb, vTb, lfT, d, dTb, jlo, qm, sg = _fox_proj_prompt(
                xp, wqkv, wmg, wf, bf, kv_all, layer=j, n_layers=depth // 2, tm=tm, tb=tb, nh=nh)
            kv_all = (kT_all, vT_all)
            mt = _fox_attn_prompt(jlo, q, kTb, vTb, d, dTb, sg)
            xp = _fox_finish_prompt(xp, mt, qm, sg, mkT_all[i], mvT_all[i], woutb, g_row, b_row,
                                    alpha=alpha, tm=tm)
            fl_p.append(jnp.transpose(lfT[:, :nh, :], (0, 2, 1)))

            qT, knT, vnT, lfnT, slT, qmT, sgT = _s_pre_fox(xsT, wqkv, wmg, wf, bf, nh=nh)
            tokT = _s_fox(page_table, qT, vnT, slT, lfnT, lf5, k5, v5, j, nh=nh)
            crsT = _s_cross(qmT, mk5, mv5, i, bc=bc)
            fk_s.append(knT.T.reshape(nb, 1, nh, HEAD_DIM))
            fv_s.append(vnT.T.reshape(nb, 1, nh, HEAD_DIM))
            fl_s.append(lfnT[:nh].T.reshape(nb, 1, nh))
        xsT = _s_post(xsT, tokT, crsT, sgT, woutb, g_col, b_col, alpha=alpha)

    n_mem = mem_prompt.shape[1]
    mk_out = jnp.transpose(mkT_all.reshape(depth, bsz, X_HEADS, HEAD_DIM, n_mem), (0, 1, 4, 2, 3))
    mv_out = jnp.transpose(mvT_all.reshape(depth, bsz, X_HEADS, HEAD_DIM, n_mem), (0, 1, 4, 2, 3))
    n_fox = depth // 2
    fk_all = jnp.transpose(kv_all[0].reshape(n_fox, bsz, nh, HEAD_DIM, seq), (0, 1, 4, 2, 3))
    fv_all = jnp.transpose(kv_all[1].reshape(n_fox, bsz, nh, HEAD_DIM, seq), (0, 1, 4, 2, 3))
    return (xp, xsT.T.reshape(nb, 1, d_model),
            fk_all, fv_all, jnp.stack(fl_p),
            jnp.stack(fk_s), jnp.stack(fv_s), jnp.stack(fl_s),
            jnp.stack(cv_p), jnp.stack(cv_s), mk_out, mv_out)
```

```python
import functools

import jax
import jax.numpy as jnp
from jax import lax
from jax.experimental import pallas as pl
from jax.experimental.pallas import tpu as pltpu

F32 = jnp.float32
BF16 = jnp.bfloat16

HEAD_DIM = 64
X_HEADS = 4
X_WIDTH = X_HEADS * HEAD_DIM
CONV_WIDTH = 3
LN_EPS = 1e-5
QK_SCALE = HEAD_DIM ** -0.5
LOG2E = 1.4426950408889634
SKIP_LOG2 = 150.0
NORM_SLACK = 1.02
MAX_HEAD_GROUP = 6
LANES = 128
VMEM_LIMIT = 56 * 1024 * 1024

_NT = (((1,), (1,)), ((), ()))


def _dot(a, b):
    return jnp.dot(a, b, preferred_element_type=F32)


def _dot_nt(a, b):
    return lax.dot_general(a, b, _NT, preferred_element_type=F32)


def _silu(g):
    return g * jax.nn.sigmoid(g)


def _log_sigmoid(x):
    return jnp.minimum(x, 0.0) - jnp.log1p(jnp.exp(-jnp.abs(x)))


def _layer_norm(y, g, b, axis):
    mu = jnp.mean(y, axis=axis, keepdims=True)
    yc = y - mu
    var = jnp.mean(yc * yc, axis=axis, keepdims=True)
    return yc * lax.rsqrt(var + LN_EPS) * g + b


def _split3(x):
    hi = x.astype(BF16)
    r1 = x - hi.astype(F32)
    mid = r1.astype(BF16)
    lo = (r1 - mid.astype(F32)).astype(BF16)
    return hi, mid, lo


def _const_spec(shape):
    nd = len(shape)
    return pl.BlockSpec(shape, lambda *_: (0,) * nd, pipeline_mode=pl.Buffered(1))


def _mem_kv_kernel(mem_ref, wT_ref, kT_ref, vT_ref):
    memb = mem_ref[0].astype(BF16)
    kvT = _dot_nt(wT_ref[0], memb)
    kT_ref[0, 0] = kvT[:X_WIDTH]
    vT_ref[0, 0] = kvT[X_WIDTH:]


def _mem_kv(mem_prompt, w_mem_kvT):
    depth = w_mem_kvT.shape[0]
    bsz, n_mem, d_model = mem_prompt.shape
    out = jax.ShapeDtypeStruct((depth, bsz, X_WIDTH, n_mem), F32)
    return pl.pallas_call(
        _mem_kv_kernel,
        grid=(depth, bsz),
        in_specs=[pl.BlockSpec((1, n_mem, d_model), lambda i, b: (b, 0, 0)),
                  pl.BlockSpec((1, 2 * X_WIDTH, d_model), lambda i, b: (i, 0, 0))],
        out_specs=[pl.BlockSpec((1, 1, X_WIDTH, n_mem), lambda i, b: (i, b, 0, 0))] * 2,
        out_shape=[out, out],
        compiler_params=pltpu.CompilerParams(dimension_semantics=("arbitrary", "arbitrary")),
        name="mem_kv",
    )(mem_prompt, w_mem_kvT)


def _cross_prompt(qm, kT, vT):
    lane = lax.broadcasted_iota(jnp.int32, qm.shape, 1)
    vrow = lax.broadcasted_iota(jnp.int32, vT.shape, 0)
    hs = range(X_HEADS)
    in_head = lambda idx, h: (idx >= h * HEAD_DIM) & (idx < (h + 1) * HEAD_DIM)
    s = [_dot(jnp.where(in_head(lane, h), qm, 0.0).astype(BF16), kT) for h in hs]
    m = [jnp.max(s[h], axis=-1, keepdims=True) for h in hs]
    e = [jnp.exp(s[h] - m[h]) for h in hs]
    p = [e[h] * (1.0 / jnp.sum(e[h], axis=-1, keepdims=True)) for h in hs]
    out = None
    for h in hs:
        vh = jnp.where(in_head(vrow, h), vT, jnp.zeros_like(vT))
        o = _dot_nt(p[h].astype(BF16), vh)
        out = o if out is None else out + o
    return out


def _finish_prompt(x, tok, crs, sg, wout_ref, g, b, alpha):
    tw = tok.shape[1]
    mt = (tok * sg[:, :tw]).astype(BF16)
    mc = (crs * sg[:, tw:]).astype(BF16)
    y = alpha * x + _dot(mt, wout_ref[0:tw, :]) + _dot(mc, wout_ref[tw:, :])
    return _layer_norm(y, g, b, axis=-1)


def _conv_layer_kernel(x_ref, win_ref, cw_ref, mkT_ref, mvT_ref, wout_ref, g_ref, b_ref,
                       xo_ref, st_ref, ubuf, *, alpha, tw):
    t = pl.program_id(1)
    tm = x_ref.shape[1]

    @pl.when(t == 0)
    def _():
        ubuf[0:8, :] = jnp.zeros((8, tw), F32)

    x = x_ref[0]
    xb = x.astype(BF16)
    bg = _dot(xb, win_ref[:, 0:tw])
    u = _dot(xb, win_ref[:, tw:2 * tw]) * _dot(xb, win_ref[:, 2 * tw:3 * tw])
    ubuf[8:8 + tm, :] = u
    cw = cw_ref[...]
    y = cw[0:1] * ubuf[6:6 + tm, :] + cw[1:2] * ubuf[7:7 + tm, :] + cw[2:3] * u
    tok = bg * y
    last = ubuf[tm + 6:tm + 8, :]
    ubuf[6:8, :] = last
    st_ref[0] = last

    qm = _dot(xb, win_ref[:, 3 * tw:3 * tw + X_WIDTH]) * QK_SCALE
    gate = _dot(xb, win_ref[:, 3 * tw + X_WIDTH:])
    crs = _cross_prompt(qm, mkT_ref[0].astype(BF16), mvT_ref[0].astype(BF16))
    xo_ref[0] = _finish_prompt(x, tok, crs, _silu(gate), wout_ref, g_ref[...], b_ref[...], alpha)


def _conv_layer_prompt(x, win, cw, mkT, mvT, wout, g, b, *, alpha, tm):
    bsz, seq, d_model = x.shape
    tw = cw.shape[1]
    n_mem = mkT.shape[2]
    kern = functools.partial(_conv_layer_kernel, alpha=alpha, tw=tw)
    return pl.pallas_call(
        kern,
        grid=(bsz, seq // tm),
        in_specs=[pl.BlockSpec((1, tm, d_model), lambda bb, t: (bb, t, 0)),
                  _const_spec(win.shape),
                  _const_spec(cw.shape),
                  pl.BlockSpec((1, X_WIDTH, n_mem), lambda bb, t: (bb, 0, 0)),
                  pl.BlockSpec((1, X_WIDTH, n_mem), lambda bb, t: (bb, 0, 0)),
                  _const_spec(wout.shape),
                  _const_spec(g.shape),
                  _const_spec(b.shape)],
        out_specs=[pl.BlockSpec((1, tm, d_model), lambda bb, t: (bb, t, 0)),
                   pl.BlockSpec((1, CONV_WIDTH - 1, tw), lambda bb, t: (bb, 0, 0))],
        out_shape=[jax.ShapeDtypeStruct(x.shape, F32),
                   jax.ShapeDtypeStruct((bsz, CONV_WIDTH - 1, tw), F32)],
        scratch_shapes=[pltpu.VMEM((tm + 8, tw), F32)],
        compiler_params=pltpu.CompilerParams(dimension_semantics=("arbitrary", "arbitrary"),
                                             vmem_limit_bytes=VMEM_LIMIT),
        name="conv_layer_prompt",
    )(x, win, cw, mkT, mvT, wout, g, b)


def _fox_proj_kernel(x_ref, wqkv_ref, wmg_ref, wf_ref, bf_ref, *rest, tw, nh, tb, n_alias):
    (q_ref, kT_ref, vT_ref, kTb_ref, vTb_ref, lfT_ref, d_ref, dTb_ref, jlo_ref, qm_ref, sg_ref,
     carry_ref, tri_ref, hk_ref, hd_ref) = rest[n_alias:]
    t = pl.program_id(1)
    tm = x_ref.shape[1]
    hp = bf_ref.shape[0]
    nsub = tm // tb

    @pl.when(t == 0)
    def _():
        carry_ref[...] = jnp.zeros(carry_ref.shape, F32)
        hk_ref[...] = jnp.zeros(hk_ref.shape, F32)
        hd_ref[...] = jnp.zeros(hd_ref.shape, F32)
        r = lax.broadcasted_iota(jnp.int32, (tm, tm), 0)
        c = lax.broadcasted_iota(jnp.int32, (tm, tm), 1)
        tri_ref[...] = jnp.where(r <= c, 1.0, 0.0).astype(BF16)

    xb = x_ref[0].astype(BF16)
    q2 = (_dot_nt(xb, wqkv_ref[0:tw, :]) * (QK_SCALE * LOG2E)).astype(BF16)
    q_ref[0] = q2
    qm_ref[0] = (_dot_nt(xb, wmg_ref[0:X_WIDTH, :]) * QK_SCALE).astype(BF16)
    sg_ref[0] = _silu(_dot_nt(xb, wmg_ref[X_WIDTH:, :])).astype(BF16)

    kT = _dot_nt(wqkv_ref[tw:2 * tw, :], xb)
    vT = _dot_nt(wqkv_ref[2 * tw:3 * tw, :], xb)
    kT_ref[0] = kT
    vT_ref[0] = vT
    kTb = kT.astype(BF16)
    vTb = vT.astype(BF16)
    for u in range(nsub):
        for h in range(nh):
            kTb_ref[0, u, h] = kTb[h * HEAD_DIM:(h + 1) * HEAD_DIM, u * tb:(u + 1) * tb]
            vTb_ref[0, u, h] = vTb[h * HEAD_DIM:(h + 1) * HEAD_DIM, u * tb:(u + 1) * tb]

    fT = _dot_nt(wf_ref[...], xb) + bf_ref[...]
    hrow = lax.broadcasted_iota(jnp.int32, (hp, tm), 0)
    lf = jnp.where(hrow < nh, _log_sigmoid(fT), 0.0)
    lfT_ref[0] = lf

    cs3 = _dot(jnp.concatenate(_split3(lf), axis=0), tri_ref[...])
    cs = cs3[0:hp] + cs3[hp:2 * hp] + cs3[2 * hp:3 * hp] + carry_ref[:, 0:1]
    carry_ref[...] = jnp.broadcast_to(cs[:, tm - 1:tm], carry_ref.shape)
    cs2 = cs * LOG2E
    for u in range(nsub):
        for h in range(hp):
            dTb_ref[0, u, h] = cs2[h:h + 1, u * tb:(u + 1) * tb]
    d3 = [t.astype(F32) for t in _split3(cs2)] + [jnp.zeros((LANES - 3 * hp, tm), F32)]
    d_ref[0] = jnp.concatenate(d3, axis=0).T.astype(BF16)

    head_of_col = lax.broadcasted_iota(jnp.int32, (LANES, tw), 1) // HEAD_DIM
    sel = jnp.where(head_of_col == lax.broadcasted_iota(jnp.int32, (LANES, tw), 0), 1.0, 0.0).astype(BF16)
    q2f = q2.astype(F32)
    nq = _dot_nt(sel, (q2f * q2f).astype(BF16))
    kf = kTb.astype(F32)
    ksq = kf * kf
    nk = jnp.concatenate([jnp.sum(ksq[h * HEAD_DIM:(h + 1) * HEAD_DIM], axis=0, keepdims=True)
                          for h in range(nh)] + [jnp.zeros((hp - nh, tm), F32)], axis=0)
    lane = lax.broadcasted_iota(jnp.int32, (hp, LANES), 1)
    lane_f = lane.astype(F32)
    for u in range(nsub):
        i = t * nsub + u
        qmax = jnp.sqrt(jnp.max(nq[0:hp, u * tb:(u + 1) * tb], axis=1, keepdims=True))
        kmax = jnp.sqrt(jnp.max(nk[:, u * tb:(u + 1) * tb], axis=1, keepdims=True))
        hk = jnp.where(lane == i, kmax, hk_ref[...])
        hd = jnp.where(lane == i, cs2[:, (u + 1) * tb - 1:(u + 1) * tb], hd_ref[...])
        hk_ref[...] = hk
        hd_ref[...] = hd
        gap = qmax * (hk + kmax) * NORM_SLACK + cs2[:, u * tb:u * tb + 1] - hd
        alive = (gap >= -SKIP_LOG2) & (lane < i)
        i_f = i.astype(F32)
        jlo_ref[0, u] = jnp.min(jnp.where(alive, lane_f, i_f), axis=1, keepdims=True).astype(jnp.int32)


def _fox_proj_prompt(x, wqkv, wmg, wf, bf, kv_prev, *, layer, n_layers, tm, tb, nh):
    bsz, seq, d_model = x.shape
    tw = nh * HEAD_DIM
    hp = bf.shape[0]
    nblk = seq // tb
    nsub = tm // tb
    assert nblk <= LANES
    aliased = [] if kv_prev is None else list(kv_prev)
    kern = functools.partial(_fox_proj_kernel, tw=tw, nh=nh, tb=tb, n_alias=len(aliased))
    row = lambda bb, t: (bb, t, 0)
    colT = lambda bb, t: (bb, 0, t)
    blocked = lambda bb, t: (bb, t, 0, 0, 0)
    kv_all = jax.ShapeDtypeStruct((n_layers, bsz, tw, seq), F32)
    kv_spec = pl.BlockSpec((None, 1, tw, tm), lambda bb, t: (layer, bb, 0, t))
    out_shape = [
        jax.ShapeDtypeStruct((bsz, seq, tw), BF16),
        kv_all,
        kv_all,
        jax.ShapeDtypeStruct((bsz, nblk, nh, HEAD_DIM, tb), BF16),
        jax.ShapeDtypeStruct((bsz, nblk, nh, HEAD_DIM, tb), BF16),
        jax.ShapeDtypeStruct((bsz, hp, seq), F32),
        jax.ShapeDtypeStruct((bsz, seq, LANES), BF16),
        jax.ShapeDtypeStruct((bsz, nblk, hp, 1, tb), F32),
        jax.ShapeDtypeStruct((bsz, nblk, hp, 1), jnp.int32),
        jax.ShapeDtypeStruct((bsz, seq, X_WIDTH), BF16),
        jax.ShapeDtypeStruct((bsz, seq, d_model), BF16),
    ]
    out_specs = [
        pl.BlockSpec((1, tm, tw), row),
        kv_spec,
        kv_spec,
        pl.BlockSpec((1, nsub, nh, HEAD_DIM, tb), blocked),
        pl.BlockSpec((1, nsub, nh, HEAD_DIM, tb), blocked),
        pl.BlockSpec((1, hp, tm), colT),
        pl.BlockSpec((1, tm, LANES), row),
        pl.BlockSpec((1, nsub, hp, 1, tb), blocked),
        pl.BlockSpec((1, nsub, hp, 1), lambda bb, t: (bb, t, 0, 0)),
        pl.BlockSpec((1, tm, X_WIDTH), row),
        pl.BlockSpec((1, tm, d_model), row),
    ]
    n_in = 5
    return pl.pallas_call(
        kern,
        grid=(bsz, seq // tm),
        in_specs=[pl.BlockSpec((1, tm, d_model), row), _const_spec(wqkv.shape), _const_spec(wmg.shape),
                  _const_spec(wf.shape),
                  _const_spec(bf.shape)] + [pl.BlockSpec(memory_space=pl.ANY)] * len(aliased),
        out_specs=out_specs,
        out_shape=out_shape,
        input_output_aliases={n_in + k: 1 + k for k in range(len(aliased))},
        scratch_shapes=[pltpu.VMEM((hp, LANES), F32), pltpu.VMEM((tm, tm), BF16),
                        pltpu.VMEM((hp, LANES), F32), pltpu.VMEM((hp, LANES), F32)],
        compiler_params=pltpu.CompilerParams(dimension_semantics=("arbitrary", "arbitrary"),
                                             vmem_limit_bytes=VMEM_LIMIT),
        name="fox_proj_prompt",
    )(x, wqkv, wmg, wf, bf, *aliased)


def _fox_attn_kernel(jlo_ref, q_ref, kTb_ref, vTb_ref, d_ref, dTb_ref, sg_ref, mt_ref,
                     qh_scr, dq_scr, m_scr, l_scr, acc_scr, tok_scr, *, nh, hp, group):
    bb = pl.program_id(0)
    i = pl.program_id(1)
    nblk = pl.num_programs(1)
    tq = q_ref.shape[1]
    tk = kTb_ref.shape[4]
    reps = tk // LANES

    d3 = d_ref[0]
    term = lax.broadcasted_iota(jnp.int32, (LANES, LANES), 0)
    for h in range(nh):
        qh_scr[h] = q_ref[0, :, h * HEAD_DIM:(h + 1) * HEAD_DIM]
        pick = (term == h) | (term == hp + h) | (term == 2 * hp + h)
        dq_scr[h] = _dot(d3, jnp.where(pick, 1.0, 0.0).astype(BF16))

    row = lax.broadcasted_iota(jnp.int32, (tq, tk), 0)
    col = lax.broadcasted_iota(jnp.int32, (tq, tk), 1)
    causal = col <= row

    def group_body(g, carry):
        heads = [g * group + k for k in range(group)]
        m_scr[...] = jnp.full(m_scr.shape, -jnp.inf, F32)
        l_scr[...] = jnp.zeros(l_scr.shape, F32)
        acc_scr[...] = jnp.zeros(acc_scr.shape, F32)
        qs = [qh_scr[h] for h in heads]
        dqts = [jnp.concatenate([dq_scr[h]] * reps, axis=1) for h in heads]

        def step(j, masked):
            for k, h in enumerate(heads):
                s = _dot(qs[k], kTb_ref[0, j, h])
                s = s + dqts[k] - dTb_ref[0, j, h]
                if masked:
                    s = jnp.where(causal, s, -jnp.inf)
                m_old = m_scr[k]
                m_new = jnp.maximum(m_old, jnp.max(s, axis=1, keepdims=True))
                a = jnp.exp2(m_old - m_new)
                p = jnp.exp2(s - jnp.concatenate([m_new] * reps, axis=1))
                l_scr[k] = a * l_scr[k] + jnp.sum(p, axis=1, keepdims=True)
                m_scr[k] = m_new
                pv = _dot_nt(p.astype(BF16), vTb_ref[0, j, h])
                acc_scr[k] = a[:, 0:HEAD_DIM] * acc_scr[k] + pv

        def loop_body(j, c):
            step(j, False)
            return c

        base = (bb * nblk + i) * hp
        j0 = jlo_ref[base + heads[0]]
        for h in heads[1:]:
            j0 = jnp.minimum(j0, jlo_ref[base + h])
        lax.fori_loop(j0, i, loop_body, 0)
        step(i, True)
        for k, h in enumerate(heads):
            tok_scr[h] = acc_scr[k] / l_scr[k][:, 0:HEAD_DIM]
        return carry

    lax.fori_loop(0, nh // group, group_body, 0)

    tok = jnp.concatenate([tok_scr[h] for h in range(nh)], axis=1)
    mt_ref[0] = (tok * sg_ref[0][:, 0:nh * HEAD_DIM].astype(F32)).astype(BF16)


def _fox_finish_kernel(x_ref, mt_ref, qm_ref, sg_ref, mkT_ref, mvT_ref, wout_ref, g_ref, b_ref, xo_ref,
                       *, alpha):
    tw = mt_ref.shape[2]
    crs = _cross_prompt(qm_ref[0].astype(F32), mkT_ref[0].astype(BF16), mvT_ref[0].astype(BF16))
    mc = (crs * sg_ref[0][:, tw:].astype(F32)).astype(BF16)
    y = alpha * x_ref[0] + _dot(mt_ref[0], wout_ref[0:tw, :]) + _dot(mc, wout_ref[tw:, :])
    xo_ref[0] = _layer_norm(y, g_ref[...], b_ref[...], axis=-1)


def _fox_finish_prompt(x, mt, qm, sg, mkT, mvT, wout, g, b, *, alpha, tm):
    bsz, seq, d_model = x.shape
    n_mem = mkT.shape[2]
    row = lambda w: pl.BlockSpec((1, tm, w), lambda bb, t: (bb, t, 0))
    per_b = pl.BlockSpec((1, X_WIDTH, n_mem), lambda bb, t: (bb, 0, 0))
    return pl.pallas_call(
        functools.partial(_fox_finish_kernel, alpha=alpha),
        grid=(bsz, seq // tm),
        in_specs=[row(d_model), row(mt.shape[2]), row(X_WIDTH), row(d_model), per_b, per_b,
                  _const_spec(wout.shape), _const_spec(g.shape), _const_spec(b.shape)],
        out_specs=row(d_model),
        out_shape=jax.ShapeDtypeStruct(x.shape, F32),
        compiler_params=pltpu.CompilerParams(dimension_semantics=("arbitrary", "arbitrary"),
                                             vmem_limit_bytes=VMEM_LIMIT),
        name="fox_finish_prompt",
    )(x, mt, qm, sg, mkT, mvT, wout, g, b)


def _fox_attn_prompt(jlo, q, kTb, vTb, d, dTb, sg):
    bsz, seq, d_model = sg.shape
    _, nblk, nh, _, tk = kTb.shape
    tq = tk
    tw = nh * HEAD_DIM
    hp = dTb.shape[2]
    group = max(k for k in range(1, MAX_HEAD_GROUP + 1) if nh % k == 0)
    kern = functools.partial(_fox_attn_kernel, nh=nh, hp=hp, group=group)
    row = lambda bb, t, jl: (bb, t, 0)
    per_b5 = lambda bb, t, jl: (bb, 0, 0, 0, 0)

    grid_spec = pltpu.PrefetchScalarGridSpec(
        num_scalar_prefetch=1,
        grid=(bsz, nblk),
        in_specs=[pl.BlockSpec((1, tq, tw), row),
                  pl.BlockSpec((1, nblk, nh, HEAD_DIM, tk), per_b5, pipeline_mode=pl.Buffered(1)),
                  pl.BlockSpec((1, nblk, nh, HEAD_DIM, tk), per_b5, pipeline_mode=pl.Buffered(1)),
                  pl.BlockSpec((1, tq, LANES), row),
                  pl.BlockSpec((1, nblk, hp, 1, tk), per_b5, pipeline_mode=pl.Buffered(1)),
                  pl.BlockSpec((1, tq, d_model), row)],
        out_specs=pl.BlockSpec((1, tq, tw), row),
        scratch_shapes=[pltpu.VMEM((nh, tq, HEAD_DIM), BF16),
                        pltpu.VMEM((nh, tq, LANES), F32),
                        pltpu.VMEM((group, tq, LANES), F32),
                        pltpu.VMEM((group, tq, LANES), F32),
                        pltpu.VMEM((group, tq, HEAD_DIM), F32),
                        pltpu.VMEM((nh, tq, HEAD_DIM), F32)],
    )
    return pl.pallas_call(
        kern,
        grid_spec=grid_spec,
        out_shape=jax.ShapeDtypeStruct((bsz, seq, tw), BF16),
        compiler_params=pltpu.CompilerParams(dimension_semantics=("arbitrary", "arbitrary"),
                                             vmem_limit_bytes=VMEM_LIMIT),
        name="fox_attn_prompt",
    )(jlo.reshape(-1), q, kTb, vTb, d, dTb, sg)


def _s_pre_conv_kernel(xT_ref, w_ref, st0_ref, st1_ref, cw_ref,
                       tokT_ref, uT_ref, qmT_ref, sgT_ref, *, tw):
    xb = xT_ref[...].T.astype(BF16)
    proj = lambda lo, hi: _dot(xb, w_ref[:, lo:hi]).T
    bg = proj(0, tw)
    u = proj(tw, 2 * tw) * proj(2 * tw, 3 * tw)
    cw = cw_ref[...]
    y = cw[:, 0:1] * st0_ref[...] + cw[:, 1:2] * st1_ref[...] + cw[:, 2:3] * u
    tokT_ref[...] = bg * y
    uT_ref[...] = u
    qmT_ref[...] = proj(3 * tw, 3 * tw + X_WIDTH) * QK_SCALE
    sgT_ref[...] = _silu(proj(3 * tw + X_WIDTH, w_ref.shape[1]))


def _s_pre_conv(xT, wT, st0T, st1T, cwT):
    d_model, nb = xT.shape
    tw = cwT.shape[0]
    f = lambda r: jax.ShapeDtypeStruct((r, nb), F32)
    return pl.pallas_call(
        functools.partial(_s_pre_conv_kernel, tw=tw),
        out_shape=[f(tw), f(tw), f(X_WIDTH), f(d_model)],
        compiler_params=pltpu.CompilerParams(vmem_limit_bytes=VMEM_LIMIT),
        name="sample_pre_conv",
    )(xT, wT, st0T, st1T, cwT)


def _s_pre_fox_kernel(xT_ref, wqkv_ref, wmg_ref, wf_ref, bf_ref,
                      qT_ref, kT_ref, vT_ref, lfT_ref, slT_ref, qmT_ref, sgT_ref, *, tw, nh):
    xb = xT_ref[...].astype(BF16)
    nb = xb.shape[1]
    hp = bf_ref.shape[0]
    q = _dot(wqkv_ref[0:tw, :], xb) * QK_SCALE
    k = _dot(wqkv_ref[tw:2 * tw, :], xb)
    qT_ref[...] = q
    kT_ref[...] = k
    vT_ref[...] = _dot(wqkv_ref[2 * tw:3 * tw, :], xb)
    qmT_ref[...] = _dot(wmg_ref[0:X_WIDTH, :], xb) * QK_SCALE
    sgT_ref[...] = _silu(_dot(wmg_ref[X_WIDTH:, :], xb))
    fT = _dot(wf_ref[...], xb) + bf_ref[...]
    hrow = lax.broadcasted_iota(jnp.int32, (hp, nb), 0)
    lfT_ref[...] = jnp.where(hrow < nh, _log_sigmoid(fT), 0.0)
    qk = q * k
    rows = [jnp.sum(qk[h * HEAD_DIM:(h + 1) * HEAD_DIM], axis=0, keepdims=True) for h in range(nh)]
    rows.append(jnp.zeros((hp - nh, nb), F32))
    slT_ref[...] = jnp.concatenate(rows, axis=0)


def _s_pre_fox(xT, wqkv, wmg, wf, bf, *, nh):
    d_model, nb = xT.shape
    tw = nh * HEAD_DIM
    hp = bf.shape[0]
    f = lambda r: jax.ShapeDtypeStruct((r, nb), F32)
    return pl.pallas_call(
        functools.partial(_s_pre_fox_kernel, tw=tw, nh=nh),
        out_shape=[f(tw), f(tw), f(tw), f(hp), f(hp), f(X_WIDTH), f(d_model)],
        compiler_params=pltpu.CompilerParams(vmem_limit_bytes=VMEM_LIMIT),
        name="sample_pre_fox",
    )(xT, wqkv, wmg, wf, bf)


def _column(ref_or_val, onehot):
    return jnp.sum(jnp.where(onehot, ref_or_val, 0.0), axis=1, keepdims=True)


def _cross_sample_cols(qcs, mks, mvs):
    pairs = [(i, h) for i in range(len(qcs)) for h in range(X_HEADS)]
    hd = lambda h: slice(h * HEAD_DIM, (h + 1) * HEAD_DIM)
    s = [jnp.sum(qcs[i][hd(h)] * mks[i][h], axis=0, keepdims=True) for i, h in pairs]
    m = [jnp.max(x, axis=1, keepdims=True) for x in s]
    e = [jnp.exp(x - y) for x, y in zip(s, m)]
    r = [1.0 / jnp.sum(x, axis=1, keepdims=True) for x in e]
    p = [x * y for x, y in zip(e, r)]
    o = [jnp.sum(x * mvs[i][h], axis=1, keepdims=True) for x, (i, h) in zip(p, pairs)]
    return [jnp.concatenate(o[i * X_HEADS:(i + 1) * X_HEADS], axis=0) for i in range(len(qcs))]


def _s_cross_kernel(qmT_ref, mk_ref, mv_ref, crsT_ref, *, bc):
    c = pl.program_id(0)

    @pl.when(c == 0)
    def _():
        crsT_ref[...] = jnp.zeros(crsT_ref.shape, F32)

    lane = lax.broadcasted_iota(jnp.int32, crsT_ref.shape, 1)
    qmT = qmT_ref[...]
    hots = [lane == c * bc + bb for bb in range(bc)]
    cols = _cross_sample_cols([_column(qmT, hot) for hot in hots],
                              [mk_ref.at[bb] for bb in range(bc)], [mv_ref.at[bb] for bb in range(bc)])
    out = crsT_ref[...]
    for bb in range(bc):
        out = jnp.where(hots[bb], cols[bb], out)
    crsT_ref[...] = out


def _s_cross(qmT, mk5, mv5, layer, *, bc):
    nb = qmT.shape[1]
    n_mem = mk5.shape[4]
    blk = pl.BlockSpec((None, bc, X_HEADS, HEAD_DIM, n_mem), lambda c: (layer, c, 0, 0, 0))
    return pl.pallas_call(
        functools.partial(_s_cross_kernel, bc=bc),
        grid=(nb // bc,),
        in_specs=[pl.BlockSpec(qmT.shape, lambda c: (0, 0)), blk, blk],
        out_specs=pl.BlockSpec(qmT.shape, lambda c: (0, 0)),
        out_shape=jax.ShapeDtypeStruct(qmT.shape, F32),
        compiler_params=pltpu.CompilerParams(dimension_semantics=("arbitrary",),
                                             vmem_limit_bytes=VMEM_LIMIT),
        name="sample_cross",
    )(qmT, mk5, mv5)


def _s_fox_scores_kernel(pt_ref, qT_ref, slT_ref, lfnT_ref, *rest, nh, n_pages, rows):
    n = rows * n_pages
    p_ref, pself_ref, l_ref, alive_ref, l_scr, s_scr = rest[2 * n:]
    for r in range(rows):
        _s_fox_scores_row(pl.program_id(0) * rows + r, qT_ref, slT_ref, lfnT_ref,
                          rest[r * n_pages:(r + 1) * n_pages], rest[n + r * n_pages:n + (r + 1) * n_pages],
                          p_ref.at[r], pself_ref.at[r], l_ref.at[r], alive_ref.at[r],
                          l_scr.at[r], s_scr.at[r], nh=nh, n_pages=n_pages)


def _s_fox_scores_row(b, qT_ref, slT_ref, lfnT_ref, lf_pages, k_pages,
                      p_ref, pself_ref, l_ref, alive_ref, l_scr, s_scr, *, nh, n_pages):
    page = k_pages[0].shape[2]
    nrow = nh * n_pages

    def onehot(shape):
        return lax.broadcasted_iota(jnp.int32, shape, 1) == b

    qc = _column(qT_ref[...], onehot(qT_ref.shape))
    slc = _column(slT_ref[...], onehot(slT_ref.shape))
    lfn = _column(lfnT_ref[...], onehot(lfnT_ref.shape))

    for p in range(n_pages):
        for h in range(nh):
            l_scr[h * n_pages + p:h * n_pages + p + 1, :] = lf_pages[p][h]
    lfp = l_scr[...]
    r = lax.broadcasted_iota(jnp.int32, (page, page), 0)
    c = lax.broadcasted_iota(jnp.int32, (page, page), 1)
    after_in_page = jnp.where(r > c, 1.0, 0.0).astype(BF16)
    w3 = _dot(jnp.concatenate(_split3(lfp), axis=0), after_in_page)
    within = w3[0:nrow] + w3[nrow:2 * nrow] + w3[2 * nrow:3 * nrow]
    tot = jnp.broadcast_to(jnp.sum(lfp, axis=1, keepdims=True), (nrow, page))
    ru = lax.broadcasted_iota(jnp.int32, (nrow, nrow), 0)
    cu = lax.broadcasted_iota(jnp.int32, (nrow, nrow), 1)
    later_page = jnp.where((cu > ru) & (cu // n_pages == ru // n_pages), 1.0, 0.0).astype(BF16)
    a3 = _dot(later_page, jnp.concatenate(_split3(tot), axis=1))
    bias = within + a3[:, 0:page] + a3[:, page:2 * page] + a3[:, 2 * page:3 * page]

    for h in range(nh):
        qh = qc[h * HEAD_DIM:(h + 1) * HEAD_DIM]
        for p in range(n_pages):
            s_scr[h * n_pages + p:h * n_pages + p + 1, :] = jnp.sum(qh * k_pages[p][h], axis=0, keepdims=True)
    s_all = s_scr[...] + bias

    hs = range(nh)
    s_h = [s_all[h * n_pages:(h + 1) * n_pages] + lfn[h:h + 1] for h in hs]
    sl = [slc[h:h + 1] for h in hs]
    m = [jnp.maximum(jnp.max(jnp.max(s_h[h], axis=1, keepdims=True), axis=0, keepdims=True), sl[h])
         for h in hs]
    p_h = [jnp.exp(s_h[h] - m[h]) for h in hs]
    p_self = [jnp.exp(sl[h] - m[h]) for h in hs]
    l = [jnp.sum(jnp.sum(p_h[h], axis=1, keepdims=True), axis=0, keepdims=True) + p_self[h] for h in hs]
    pm = [jnp.max(p_h[h], axis=1, keepdims=True) for h in hs]
    page_max = pm[0]
    for h in hs:
        p_ref[h * n_pages:(h + 1) * n_pages, :] = p_h[h]
        pself_ref[h:h + 1, :] = jnp.broadcast_to(p_self[h], (1, LANES))
        l_ref[h:h + 1, :] = jnp.broadcast_to(l[h], (1, LANES))
        page_max = jnp.maximum(page_max, pm[h])
    hp = pself_ref.shape[0]
    if hp > nh:
        pself_ref[nh:hp, :] = jnp.zeros((hp - nh, LANES), F32)
        l_ref[nh:hp, :] = jnp.ones((hp - nh, LANES), F32)
    alive_ref[...] = jnp.broadcast_to(page_max, (n_pages, LANES))


def _s_fox_values_kernel(pt_ref, p_ref, pself_ref, l_ref, vnT_ref, *rest, nh, n_pages, rows):
    v_pages = rest[0:rows * n_pages]
    tokT_ref = rest[rows * n_pages]
    c = pl.program_id(0)

    @pl.when(c == 0)
    def _():
        tokT_ref[...] = jnp.zeros(tokT_ref.shape, F32)

    lane = lax.broadcasted_iota(jnp.int32, vnT_ref.shape, 1)
    lane_h = lax.broadcasted_iota(jnp.int32, (HEAD_DIM, tokT_ref.shape[1]), 1)
    for r in range(rows):
        b = c * rows + r
        vnc = _column(vnT_ref[...], lane == b)
        sums = []
        for h in range(nh):
            acc = p_ref[r, h * n_pages:h * n_pages + 1, :] * v_pages[r * n_pages][h]
            for p in range(1, n_pages):
                acc = acc + p_ref[r, h * n_pages + p:h * n_pages + p + 1, :] * v_pages[r * n_pages + p][h]
            sums.append(jnp.sum(acc, axis=1, keepdims=True))
        for h in range(nh):
            hd = slice(h * HEAD_DIM, (h + 1) * HEAD_DIM)
            o = (sums[h] + pself_ref[r, h:h + 1, 0:1] * vnc[hd]) / l_ref[r, h:h + 1, 0:1]
            tokT_ref[hd, :] = jnp.where(lane_h == b, o, tokT_ref[hd, :])


def _s_fox(page_table, qT, vnT, slT, lfnT, lf5, k5, v5, fox_layer, *, nh):
    tw, nb = qT.shape
    n_pages = page_table.shape[1]
    page = k5.shape[4]
    hp = slT.shape[0]
    nrow = nh * n_pages
    cparams = pltpu.CompilerParams(dimension_semantics=("arbitrary",), vmem_limit_bytes=VMEM_LIMIT)

    rows = 2 if nb % 2 == 0 else 1
    slots = [(r, p) for r in range(rows) for p in range(n_pages)]
    full = lambda a: pl.BlockSpec(a.shape, lambda c, pt: (0, 0))
    per_step = lambda n, w: pl.BlockSpec((rows, n, w), lambda c, pt: (c, 0, 0))
    kv_specs = [pl.BlockSpec((None, None, nh, HEAD_DIM, page),
                             lambda c, pt, r=r, p=p: (fox_layer, pt[c * rows + r, p], 0, 0, 0)) for r, p in slots]

    lf_specs = [pl.BlockSpec((None, nh, None, 1, page),
                             lambda c, pt, r=r, p=p: (fox_layer, 0, pt[c * rows + r, p], 0, 0)) for r, p in slots]
    probs, p_self, l_sum, page_max = pl.pallas_call(
        functools.partial(_s_fox_scores_kernel, nh=nh, n_pages=n_pages, rows=rows),
        grid_spec=pltpu.PrefetchScalarGridSpec(
            num_scalar_prefetch=1,
            grid=(nb // rows,),
            in_specs=[full(qT), full(slT), full(lfnT)] + lf_specs + kv_specs,
            out_specs=[per_step(nrow, page), per_step(hp, LANES), per_step(hp, LANES),
                       per_step(n_pages, LANES)],
            scratch_shapes=[pltpu.VMEM((rows, nrow, page), F32), pltpu.VMEM((rows, nrow, page), F32)]),
        out_shape=[jax.ShapeDtypeStruct((nb, nrow, page), F32),
                   jax.ShapeDtypeStruct((nb, hp, LANES), F32),
                   jax.ShapeDtypeStruct((nb, hp, LANES), F32),
                   jax.ShapeDtypeStruct((nb, n_pages, LANES), F32)],
        compiler_params=cparams,
        name="sample_fox_scores",
    )(page_table, qT, slT, lfnT, *([lf5] * len(slots)), *([k5] * len(slots)))

    alive = (page_max[:, :, 0] > 0.0).reshape(nb // rows, rows, n_pages)
    step_id = jnp.arange(nb // rows, dtype=jnp.int32)[:, None, None]
    last_live = jnp.maximum(lax.cummax(jnp.where(alive, step_id, -1), axis=0), 0)
    pt_eff = jnp.take_along_axis(page_table.reshape(nb // rows, rows, n_pages), last_live, axis=0)

    return pl.pallas_call(
        functools.partial(_s_fox_values_kernel, nh=nh, n_pages=n_pages, rows=rows),
        grid_spec=pltpu.PrefetchScalarGridSpec(
            num_scalar_prefetch=1,
            grid=(nb // rows,),
            in_specs=[per_step(nrow, page), per_step(hp, LANES), per_step(hp, LANES), full(vnT)] + kv_specs,
            out_specs=pl.BlockSpec((tw, nb), lambda c, pt: (0, 0))),
        out_shape=jax.ShapeDtypeStruct((tw, nb), F32),
        compiler_params=cparams,
        name="sample_fox_values",
    )(pt_eff.reshape(nb, n_pages), probs, p_self, l_sum, vnT, *([v5] * len(slots)))


def _s_post_kernel(xT_ref, tokT_ref, crsT_ref, sgT_ref, wo_ref, g_ref, b_ref, out_ref, *, alpha):
    tw = tokT_ref.shape[0]
    sg = sgT_ref[...]
    mt = (tokT_ref[...] * sg[0:tw]).T.astype(BF16)
    mc = (crsT_ref[...] * sg[tw:]).T.astype(BF16)
    y = alpha * xT_ref[...] + (_dot(mt, wo_ref[0:tw, :]) + _dot(mc, wo_ref[tw:, :])).T
    out_ref[...] = _layer_norm(y, g_ref[...], b_ref[...], axis=0)


def _s_post(xT, tokT, crsT, sgT, woT, gcol, bcol, *, alpha):
    return pl.pallas_call(
        functools.partial(_s_post_kernel, alpha=alpha),
        out_shape=jax.ShapeDtypeStruct(xT.shape, F32),
        compiler_params=pltpu.CompilerParams(vmem_limit_bytes=VMEM_LIMIT),
        name="sample_post",
    )(xT, tokT, crsT, sgT, woT, gcol, bcol)


def kernel(x_prompt, x_sample, mem_prompt, cache_fox_k, cache_fox_v, cache_fox_logf, state_conv,
           cache_mem_k, cache_mem_v, page_table, w_in_conv, conv_w, w_in_fox, b_forget, w_mem_kv,
           w_out, ln_g, ln_b):
    depth = w_out.shape[0]
    alpha = float((2 * depth) ** 0.25)
    bsz, seq, d_model = x_prompt.shape
    nb = x_sample.shape[0]
    tw = conv_w.shape[2]
    nh = tw // HEAD_DIM
    hp = -(-nh // 8) * 8
    tm = min(1024, seq)
    tb = min(256, seq)
    bc = min(8, nb)
    kv_all = None

    mkT_all, mvT_all = _mem_kv(mem_prompt, jnp.swapaxes(w_mem_kv, 1, 2).astype(BF16))
    mk5 = jnp.transpose(cache_mem_k, (0, 1, 3, 4, 2))
    mv5 = jnp.transpose(cache_mem_v, (0, 1, 3, 4, 2))
    k5 = jnp.transpose(cache_fox_k, (0, 1, 3, 4, 2))
    v5 = jnp.transpose(cache_fox_v, (0, 1, 3, 4, 2))
    lf5 = jnp.transpose(cache_fox_logf, (0, 3, 1, 2))[:, :, :, None, :]

    xp = x_prompt
    xsT = x_sample.reshape(nb, d_model).T
    fl_p, fk_s, fv_s, fl_s, cv_p, cv_s = [], [], [], [], [], []

    for i in range(depth):
        j = i // 2
        woutb = w_out[i].astype(BF16)
        g_row, b_row = ln_g[i][None, :], ln_b[i][None, :]
        g_col, b_col = ln_g[i][:, None], ln_b[i][:, None]
        if i % 2 == 0:
            wb = w_in_conv[j].astype(BF16)
            xp, st_p = _conv_layer_prompt(xp, wb, conv_w[j], mkT_all[i], mvT_all[i],
                                          woutb, g_row, b_row, alpha=alpha, tm=tm)
            cv_p.append(st_p)
            st = state_conv[j]
            tokT, uT, qmT, sgT = _s_pre_conv(xsT, wb, st[:, 0].T, st[:, 1].T, conv_w[j].T)
            cv_s.append(jnp.stack([st[:, 1], uT.T], axis=1))
            crsT = _s_cross(qmT, mk5, mv5, i, bc=bc)
        else:
            wt = w_in_fox[j].T.astype(BF16)
            o_f, o_qm = 3 * tw, 3 * tw + nh
            wqkv, wmg = wt[0:o_f], wt[o_qm:]
            wf = jnp.pad(wt[o_f:o_qm], ((0, hp - nh), (0, 0)))
            bf = jnp.pad(b_forget[j], (0, hp - nh))[:, None]
            q, kT_all, vT_all, kTb, vTb, lfT, d, dTb, jlo, qm, sg = _fox_proj_prompt(
                xp, wqkv, wmg, wf, bf, kv_all, layer=j, n_layers=depth // 2, tm=tm, tb=tb, nh=nh)
            kv_all = (kT_all, vT_all)
            mt = _fox_attn_prompt(jlo, q, kTb, vTb, d, dTb, sg)
            xp = _fox_finish_prompt(xp, mt, qm, sg, mkT_all[i], mvT_all[i], woutb, g_row, b_row,
                                    alpha=alpha, tm=tm)
            fl_p.append(jnp.transpose(lfT[:, :nh, :], (0, 2, 1)))

            qT, knT, vnT, lfnT, slT, qmT, sgT = _s_pre_fox(xsT, wqkv, wmg, wf, bf, nh=nh)
            tokT = _s_fox(page_table, qT, vnT, slT, lfnT, lf5, k5, v5, j, nh=nh)
            crsT = _s_cross(qmT, mk5, mv5, i, bc=bc)
            fk_s.append(knT.T.reshape(nb, 1, nh, HEAD_DIM))
            fv_s.append(vnT.T.reshape(nb, 1, nh, HEAD_DIM))
            fl_s.append(lfnT[:nh].T.reshape(nb, 1, nh))
        xsT = _s_post(xsT, tokT, crsT, sgT, woutb, g_col, b_col, alpha=alpha)

    n_mem = mem_prompt.shape[1]
    mk_out = jnp.transpose(mkT_all.reshape(depth, bsz, X_HEADS, HEAD_DIM, n_mem), (0, 1, 4, 2, 3))
    mv_out = jnp.transpose(mvT_all.reshape(depth, bsz, X_HEADS, HEAD_DIM, n_mem), (0, 1, 4, 2, 3))
    n_fox = depth // 2
    fk_all = jnp.transpose(kv_all[0].reshape(n_fox, bsz, nh, HEAD_DIM, seq), (0, 1, 4, 2, 3))
    fv_all = jnp.transpose(kv_all[1].reshape(n_fox, bsz, nh, HEAD_DIM, seq), (0, 1, 4, 2, 3))
    return (xp, xsT.T.reshape(nb, 1, d_model),
            fk_all, fv_all, jnp.stack(fl_p),
            jnp.stack(fk_s), jnp.stack(fv_s), jnp.stack(fl_s),
            jnp.stack(cv_p), jnp.stack(cv_s), mk_out, mv_out)
```

```python
import functools

import jax
import jax.numpy as jnp
from jax import lax
from jax.experimental import pallas as pl
from jax.experimental.pallas import tpu as pltpu

F32 = jnp.float32
BF16 = jnp.bfloat16

HEAD_DIM = 64
X_HEADS = 4
X_WIDTH = X_HEADS * HEAD_DIM
CONV_WIDTH = 3
LN_EPS = 1e-5
QK_SCALE = HEAD_DIM ** -0.5
LOG2E = 1.4426950408889634
SKIP_LOG2 = 150.0
NORM_SLACK = 1.02
MAX_HEAD_GROUP = 6
LANES = 128
VMEM_LIMIT = 56 * 1024 * 1024

_NT = (((1,), (1,)), ((), ()))


def _dot(a, b):
    return jnp.dot(a, b, preferred_element_type=F32)


def _dot_nt(a, b):
    return lax.dot_general(a, b, _NT, preferred_element_type=F32)


def _silu(g):
    return g * jax.nn.sigmoid(g)


def _log_sigmoid(x):
    return jnp.minimum(x, 0.0) - jnp.log1p(jnp.exp(-jnp.abs(x)))


def _layer_norm(y, g, b, axis):
    mu = jnp.mean(y, axis=axis, keepdims=True)
    yc = y - mu
    var = jnp.mean(yc * yc, axis=axis, keepdims=True)
    return yc * lax.rsqrt(var + LN_EPS) * g + b


def _split3(x):
    hi = x.astype(BF16)
    r1 = x - hi.astype(F32)
    mid = r1.astype(BF16)
    lo = (r1 - mid.astype(F32)).astype(BF16)
    return hi, mid, lo


def _const_spec(shape):
    nd = len(shape)
    return pl.BlockSpec(shape, lambda *_: (0,) * nd, pipeline_mode=pl.Buffered(1))


def _mem_kv_kernel(mem_ref, wT_ref, kT_ref, vT_ref):
    memb = mem_ref[0].astype(BF16)
    kvT = _dot_nt(wT_ref[0], memb)
    kT_ref[0, 0] = kvT[:X_WIDTH]
    vT_ref[0, 0] = kvT[X_WIDTH:]


def _mem_kv(mem_prompt, w_mem_kvT):
    depth = w_mem_kvT.shape[0]
    bsz, n_mem, d_model = mem_prompt.shape
    out = jax.ShapeDtypeStruct((depth, bsz, X_WIDTH, n_mem), F32)
    return pl.pallas_call(
        _mem_kv_kernel,
        grid=(depth, bsz),
        in_specs=[pl.BlockSpec((1, n_mem, d_model), lambda i, b: (b, 0, 0)),
                  pl.BlockSpec((1, 2 * X_WIDTH, d_model), lambda i, b: (i, 0, 0))],
        out_specs=[pl.BlockSpec((1, 1, X_WIDTH, n_mem), lambda i, b: (i, b, 0, 0))] * 2,
        out_shape=[out, out],
        compiler_params=pltpu.CompilerParams(dimension_semantics=("arbitrary", "arbitrary")),
        name="mem_kv",
    )(mem_prompt, w_mem_kvT)


def _cross_prompt(qm, kT, vT):
    lane = lax.broadcasted_iota(jnp.int32, qm.shape, 1)
    vrow = lax.broadcasted_iota(jnp.int32, vT.shape, 0)
    hs = range(X_HEADS)
    in_head = lambda idx, h: (idx >= h * HEAD_DIM) & (idx < (h + 1) * HEAD_DIM)
    s = [_dot(jnp.where(in_head(lane, h), qm, 0.0).astype(BF16), kT) for h in hs]
    m = [jnp.max(s[h], axis=-1, keepdims=True) for h in hs]
    e = [jnp.exp(s[h] - m[h]) for h in hs]
    p = [e[h] * (1.0 / jnp.sum(e[h], axis=-1, keepdims=True)) for h in hs]
    out = None
    for h in hs:
        vh = jnp.where(in_head(vrow, h), vT, jnp.zeros_like(vT))
        o = _dot_nt(p[h].astype(BF16), vh)
        out = o if out is None else out + o
    return out


def _finish_prompt(x, tok, crs, sg, wout_ref, g, b, alpha):
    tw = tok.shape[1]
    mt = (tok * sg[:, :tw]).astype(BF16)
    mc = (crs * sg[:, tw:]).astype(BF16)
    y = alpha * x + _dot(mt, wout_ref[0:tw, :]) + _dot(mc, wout_ref[tw:, :])
    return _layer_norm(y, g, b, axis=-1)


def _conv_layer_kernel(x_ref, win_ref, cw_ref, mkT_ref, mvT_ref, wout_ref, g_ref, b_ref,
                       xo_ref, st_ref, ubuf, *, alpha, tw):
    t = pl.program_id(1)
    tm = x_ref.shape[1]

    @pl.when(t == 0)
    def _():
        ubuf[0:8, :] = jnp.zeros((8, tw), F32)

    x = x_ref[0]
    xb = x.astype(BF16)
    bg = _dot(xb, win_ref[:, 0:tw])
    u = _dot(xb, win_ref[:, tw:2 * tw]) * _dot(xb, win_ref[:, 2 * tw:3 * tw])
    ubuf[8:8 + tm, :] = u
    cw = cw_ref[...]
    y = cw[0:1] * ubuf[6:6 + tm, :] + cw[1:2] * ubuf[7:7 + tm, :] + cw[2:3] * u
    tok = bg * y
    last = ubuf[tm + 6:tm + 8, :]
    ubuf[6:8, :] = last
    st_ref[0] = last

    qm = _dot(xb, win_ref[:, 3 * tw:3 * tw + X_WIDTH]) * QK_SCALE
    gate = _dot(xb, win_ref[:, 3 * tw + X_WIDTH:])
    crs = _cross_prompt(qm, mkT_ref[0].astype(BF16), mvT_ref[0].astype(BF16))
    xo_ref[0] = _finish_prompt(x, tok, crs, _silu(gate), wout_ref, g_ref[...], b_ref[...], alpha)


def _conv_layer_prompt(x, win, cw, mkT, mvT, wout, g, b, *, alpha, tm):
    bsz, seq, d_model = x.shape
    tw = cw.shape[1]
    n_mem = mkT.shape[2]
    kern = functools.partial(_conv_layer_kernel, alpha=alpha, tw=tw)
    return pl.pallas_call(
        kern,
        grid=(bsz, seq // tm),
        in_specs=[pl.BlockSpec((1, tm, d_model), lambda bb, t: (bb, t, 0)),
                  _const_spec(win.shape),
                  _const_spec(cw.shape),
                  pl.BlockSpec((1, X_WIDTH, n_mem), lambda bb, t: (bb, 0, 0)),
                  pl.BlockSpec((1, X_WIDTH, n_mem), lambda bb, t: (bb, 0, 0)),
                  _const_spec(wout.shape),
                  _const_spec(g.shape),
                  _const_spec(b.shape)],
        out_specs=[pl.BlockSpec((1, tm, d_model), lambda bb, t: (bb, t, 0)),
                   pl.BlockSpec((1, CONV_WIDTH - 1, tw), lambda bb, t: (bb, 0, 0))],
        out_shape=[jax.ShapeDtypeStruct(x.shape, F32),
                   jax.ShapeDtypeStruct((bsz, CONV_WIDTH - 1, tw), F32)],
        scratch_shapes=[pltpu.VMEM((tm + 8, tw), F32)],
        compiler_params=pltpu.CompilerParams(dimension_semantics=("arbitrary", "arbitrary"),
                                             vmem_limit_bytes=VMEM_LIMIT),
        name="conv_layer_prompt",
    )(x, win, cw, mkT, mvT, wout, g, b)


def _fox_proj_kernel(x_ref, wqkv_ref, wmg_ref, wf_ref, bf_ref, *rest, tw, nh, tb, n_alias):
    (q_ref, kT_ref, vT_ref, kTb_ref, vTb_ref, lfT_ref, d_ref, dTb_ref, jlo_ref, qm_ref, sg_ref,
     carry_ref, tri_ref, hk_ref, hd_ref) = rest[n_alias:]
    t = pl.program_id(1)
    tm = x_ref.shape[1]
    hp = bf_ref.shape[0]
    nsub = tm // tb

    @pl.when(t == 0)
    def _():
        carry_ref[...] = jnp.zeros(carry_ref.shape, F32)
        hk_ref[...] = jnp.zeros(hk_ref.shape, F32)
        hd_ref[...] = jnp.zeros(hd_ref.shape, F32)
        r = lax.broadcasted_iota(jnp.int32, (tm, tm), 0)
        c = lax.broadcasted_iota(jnp.int32, (tm, tm), 1)
        tri_ref[...] = jnp.where(r <= c, 1.0, 0.0).astype(BF16)

    xb = x_ref[0].astype(BF16)
    q2 = (_dot_nt(xb, wqkv_ref[0:tw, :]) * (QK_SCALE * LOG2E)).astype(BF16)
    q_ref[0] = q2
    qm_ref[0] = (_dot_nt(xb, wmg_ref[0:X_WIDTH, :]) * QK_SCALE).astype(BF16)
    sg_ref[0] = _silu(_dot_nt(xb, wmg_ref[X_WIDTH:, :])).astype(BF16)

    kT = _dot_nt(wqkv_ref[tw:2 * tw, :], xb)
    vT = _dot_nt(wqkv_ref[2 * tw:3 * tw, :], xb)
    kT_ref[0] = kT
    vT_ref[0] = vT
    kTb = kT.astype(BF16)
    vTb = vT.astype(BF16)
    for u in range(nsub):
        for h in range(nh):
            kTb_ref[0, u, h] = kTb[h * HEAD_DIM:(h + 1) * HEAD_DIM, u * tb:(u + 1) * tb]
            vTb_ref[0, u, h] = vTb[h * HEAD_DIM:(h + 1) * HEAD_DIM, u * tb:(u + 1) * tb]

    fT = _dot_nt(wf_ref[...], xb) + bf_ref[...]
    hrow = lax.broadcasted_iota(jnp.int32, (hp, tm), 0)
    lf = jnp.where(hrow < nh, _log_sigmoid(fT), 0.0)
    lfT_ref[0] = lf

    cs3 = _dot(jnp.concatenate(_split3(lf), axis=0), tri_ref[...])
    cs = cs3[0:hp] + cs3[hp:2 * hp] + cs3[2 * hp:3 * hp] + carry_ref[:, 0:1]
    carry_ref[...] = jnp.broadcast_to(cs[:, tm - 1:tm], carry_ref.shape)
    cs2 = cs * LOG2E
    for u in range(nsub):
        for h in range(hp):
            dTb_ref[0, u, h] = cs2[h:h + 1, u * tb:(u + 1) * tb]
    d3 = [t.astype(F32) for t in _split3(cs2)] + [jnp.zeros((LANES - 3 * hp, tm), F32)]
    d_ref[0] = jnp.concatenate(d3, axis=0).T.astype(BF16)

    head_of_col = lax.broadcasted_iota(jnp.int32, (LANES, tw), 1) // HEAD_DIM
    sel = jnp.where(head_of_col == lax.broadcasted_iota(jnp.int32, (LANES, tw), 0), 1.0, 0.0).astype(BF16)
    q2f = q2.astype(F32)
    nq = _dot_nt(sel, (q2f * q2f).astype(BF16))
    kf = kTb.astype(F32)
    ksq = kf * kf
    nk = jnp.concatenate([jnp.sum(ksq[h * HEAD_DIM:(h + 1) * HEAD_DIM], axis=0, keepdims=True)
                          for h in range(nh)] + [jnp.zeros((hp - nh, tm), F32)], axis=0)
    lane = lax.broadcasted_iota(jnp.int32, (hp, LANES), 1)
    lane_f = lane.astype(F32)
    for u in range(nsub):
        i = t * nsub + u
        qmax = jnp.sqrt(jnp.max(nq[0:hp, u * tb:(u + 1) * tb], axis=1, keepdims=True))
        kmax = jnp.sqrt(jnp.max(nk[:, u * tb:(u + 1) * tb], axis=1, keepdims=True))
        hk = jnp.where(lane == i, kmax, hk_ref[...])
        hd = jnp.where(lane == i, cs2[:, (u + 1) * tb - 1:(u + 1) * tb], hd_ref[...])
        hk_ref[...] = hk
        hd_ref[...] = hd
        gap = qmax * (hk + kmax) * NORM_SLACK + cs2[:, u * tb:u * tb + 1] - hd
        alive = (gap >= -SKIP_LOG2) & (lane < i)
        i_f = i.astype(F32)
        jlo_ref[0, u] = jnp.min(jnp.where(alive, lane_f, i_f), axis=1, keepdims=True).astype(jnp.int32)


def _fox_proj_prompt(x, wqkv, wmg, wf, bf, kv_prev, *, layer, n_layers, tm, tb, nh):
    bsz, seq, d_model = x.shape
    tw = nh * HEAD_DIM
    hp = bf.shape[0]
    nblk = seq // tb
    nsub = tm // tb
    assert nblk <= LANES
    aliased = [] if kv_prev is None else list(kv_prev)
    kern = functools.partial(_fox_proj_kernel, tw=tw, nh=nh, tb=tb, n_alias=len(aliased))
    row = lambda bb, t: (bb, t, 0)
    colT = lambda bb, t: (bb, 0, t)
    blocked = lambda bb, t: (bb, t, 0, 0, 0)
    kv_all = jax.ShapeDtypeStruct((n_layers, bsz, tw, seq), F32)
    kv_spec = pl.BlockSpec((None, 1, tw, tm), lambda bb, t: (layer, bb, 0, t))
    out_shape = [
        jax.ShapeDtypeStruct((bsz, seq, tw), BF16),
        kv_all,
        kv_all,
        jax.ShapeDtypeStruct((bsz, nblk, nh, HEAD_DIM, tb), BF16),
        jax.ShapeDtypeStruct((bsz, nblk, nh, HEAD_DIM, tb), BF16),
        jax.ShapeDtypeStruct((bsz, hp, seq), F32),
        jax.ShapeDtypeStruct((bsz, seq, LANES), BF16),
        jax.ShapeDtypeStruct((bsz, nblk, hp, 1, tb), F32),
        jax.ShapeDtypeStruct((bsz, nblk, hp, 1), jnp.int32),
        jax.ShapeDtypeStruct((bsz, seq, X_WIDTH), BF16),
        jax.ShapeDtypeStruct((bsz, seq, d_model), BF16),
    ]
    out_specs = [
        pl.BlockSpec((1, tm, tw), row),
        kv_spec,
        kv_spec,
        pl.BlockSpec((1, nsub, nh, HEAD_DIM, tb), blocked),
        pl.BlockSpec((1, nsub, nh, HEAD_DIM, tb), blocked),
        pl.BlockSpec((1, hp, tm), colT),
        pl.BlockSpec((1, tm, LANES), row),
        pl.BlockSpec((1, nsub, hp, 1, tb), blocked),
        pl.BlockSpec((1, nsub, hp, 1), lambda bb, t: (bb, t, 0, 0)),
        pl.BlockSpec((1, tm, X_WIDTH), row),
        pl.BlockSpec((1, tm, d_model), row),
    ]
    n_in = 5
    return pl.pallas_call(
        kern,
        grid=(bsz, seq // tm),
        in_specs=[pl.BlockSpec((1, tm, d_model), row), _const_spec(wqkv.shape), _const_spec(wmg.shape),
                  _const_spec(wf.shape),
                  _const_spec(bf.shape)] + [pl.BlockSpec(memory_space=pl.ANY)] * len(aliased),
        out_specs=out_specs,
        out_shape=out_shape,
        input_output_aliases={n_in + k: 1 + k for k in range(len(aliased))},
        scratch_shapes=[pltpu.VMEM((hp, LANES), F32), pltpu.VMEM((tm, tm), BF16),
                        pltpu.VMEM((hp, LANES), F32), pltpu.VMEM((hp, LANES), F32)],
        compiler_params=pltpu.CompilerParams(dimension_semantics=("arbitrary", "arbitrary"),
                                             vmem_limit_bytes=VMEM_LIMIT),
        name="fox_proj_prompt",
    )(x, wqkv, wmg, wf, bf, *aliased)


def _fox_attn_kernel(jlo_ref, q_ref, kTb_ref, vTb_ref, d_ref, dTb_ref, sg_ref, mt_ref,
                     qh_scr, dq_scr, m_scr, l_scr, acc_scr, tok_scr, *, nh, hp, group):
    bb = pl.program_id(0)
    i = pl.program_id(1)
    nblk = pl.num_programs(1)
    tq = q_ref.shape[1]
    tk = kTb_ref.shape[4]
    reps = tk // LANES

    d3 = d_ref[0]
    term = lax.broadcasted_iota(jnp.int32, (LANES, LANES), 0)
    for h in range(nh):
        qh_scr[h] = q_ref[0, :, h * HEAD_DIM:(h + 1) * HEAD_DIM]
        pick = (term == h) | (term == hp + h) | (term == 2 * hp + h)
        dq_scr[h] = _dot(d3, jnp.where(pick, 1.0, 0.0).astype(BF16))

    row = lax.broadcasted_iota(jnp.int32, (tq, tk), 0)
    col = lax.broadcasted_iota(jnp.int32, (tq, tk), 1)
    causal = col <= row

    def group_body(g, carry):
        heads = [g * group + k for k in range(group)]
        m_scr[...] = jnp.full(m_scr.shape, -jnp.inf, F32)
        l_scr[...] = jnp.zeros(l_scr.shape, F32)
        acc_scr[...] = jnp.zeros(acc_scr.shape, F32)
        qs = [qh_scr[h] for h in heads]
        dqts = [jnp.concatenate([dq_scr[h]] * reps, axis=1) for h in heads]

        def step(j, masked):
            for k, h in enumerate(heads):
                s = _dot(qs[k], kTb_ref[0, j, h])
                s = s + dqts[k] - dTb_ref[0, j, h]
                if masked:
                    s = jnp.where(causal, s, -jnp.inf)
                m_old = m_scr[k]
                m_new = jnp.maximum(m_old, jnp.max(s, axis=1, keepdims=True))
                a = jnp.exp2(m_old - m_new)
                p = jnp.exp2(s - jnp.concatenate([m_new] * reps, axis=1))
                l_scr[k] = a * l_scr[k] + jnp.sum(p, axis=1, keepdims=True)
                m_scr[k] = m_new
                pv = _dot_nt(p.astype(BF16), vTb_ref[0, j, h])
                acc_scr[k] = a[:, 0:HEAD_DIM] * acc_scr[k] + pv

        def loop_body(j, c):
            step(j, False)
            return c

        base = (bb * nblk + i) * hp
        j0 = jlo_ref[base + heads[0]]
        for h in heads[1:]:
            j0 = jnp.minimum(j0, jlo_ref[base + h])
        lax.fori_loop(j0, i, loop_body, 0)
        step(i, True)
        for k, h in enumerate(heads):
            tok_scr[h] = acc_scr[k] / l_scr[k][:, 0:HEAD_DIM]
        return carry

    lax.fori_loop(0, nh // group, group_body, 0)

    tok = jnp.concatenate([tok_scr[h] for h in range(nh)], axis=1)
    mt_ref[0] = (tok * sg_ref[0][:, 0:nh * HEAD_DIM].astype(F32)).astype(BF16)


def _fox_finish_kernel(x_ref, mt_ref, qm_ref, sg_ref, mkT_ref, mvT_ref, wout_ref, g_ref, b_ref, xo_ref,
                       *, alpha):
    tw = mt_ref.shape[2]
    crs = _cross_prompt(qm_ref[0].astype(F32), mkT_ref[0].astype(BF16), mvT_ref[0].astype(BF16))
    mc = (crs * sg_ref[0][:, tw:].astype(F32)).astype(BF16)
    y = alpha * x_ref[0] + _dot(mt_ref[0], wout_ref[0:tw, :]) + _dot(mc, wout_ref[tw:, :])
    xo_ref[0] = _layer_norm(y, g_ref[...], b_ref[...], axis=-1)


def _fox_finish_prompt(x, mt, qm, sg, mkT, mvT, wout, g, b, *, alpha, tm):
    bsz, seq, d_model = x.shape
    n_mem = mkT.shape[2]
    row = lambda w: pl.BlockSpec((1, tm, w), lambda bb, t: (bb, t, 0))
    per_b = pl.BlockSpec((1, X_WIDTH, n_mem), lambda bb, t: (bb, 0, 0))
    return pl.pallas_call(
        functools.partial(_fox_finish_kernel, alpha=alpha),
        grid=(bsz, seq // tm),
        in_specs=[row(d_model), row(mt.shape[2]), row(X_WIDTH), row(d_model), per_b, per_b,
                  _const_spec(wout.shape), _const_spec(g.shape), _const_spec(b.shape)],
        out_specs=row(d_model),
        out_shape=jax.ShapeDtypeStruct(x.shape, F32),
        compiler_params=pltpu.CompilerParams(dimension_semantics=("arbitrary", "arbitrary"),
                                             vmem_limit_bytes=VMEM_LIMIT),
        name="fox_finish_prompt",
    )(x, mt, qm, sg, mkT, mvT, wout, g, b)


def _fox_attn_prompt(jlo, q, kTb, vTb, d, dTb, sg):
    bsz, seq, d_model = sg.shape
    _, nblk, nh, _, tk = kTb.shape
    tq = tk
    tw = nh * HEAD_DIM
    hp = dTb.shape[2]
    group = max(k for k in range(1, MAX_HEAD_GROUP + 1) if nh % k == 0)
    kern = functools.partial(_fox_attn_kernel, nh=nh, hp=hp, group=group)
    row = lambda bb, t, jl: (bb, t, 0)
    per_b5 = lambda bb, t, jl: (bb, 0, 0, 0, 0)

    grid_spec = pltpu.PrefetchScalarGridSpec(
        num_scalar_prefetch=1,
        grid=(bsz, nblk),
        in_specs=[pl.BlockSpec((1, tq, tw), row),
                  pl.BlockSpec((1, nblk, nh, HEAD_DIM, tk), per_b5, pipeline_mode=pl.Buffered(1)),
                  pl.BlockSpec((1, nblk, nh, HEAD_DIM, tk), per_b5, pipeline_mode=pl.Buffered(1)),
                  pl.BlockSpec((1, tq, LANES), row),
                  pl.BlockSpec((1, nblk, hp, 1, tk), per_b5, pipeline_mode=pl.Buffered(1)),
                  pl.BlockSpec((1, tq, d_model), row)],
        out_specs=pl.BlockSpec((1, tq, tw), row),
        scratch_shapes=[pltpu.VMEM((nh, tq, HEAD_DIM), BF16),
                        pltpu.VMEM((nh, tq, LANES), F32),
                        pltpu.VMEM((group, tq, LANES), F32),
                        pltpu.VMEM((group, tq, LANES), F32),
                        pltpu.VMEM((group, tq, HEAD_DIM), F32),
                        pltpu.VMEM((nh, tq, HEAD_DIM), F32)],
    )
    return pl.pallas_call(
        kern,
        grid_spec=grid_spec,
        out_shape=jax.ShapeDtypeStruct((bsz, seq, tw), BF16),
        compiler_params=pltpu.CompilerParams(dimension_semantics=("arbitrary", "arbitrary"),
                                             vmem_limit_bytes=VMEM_LIMIT),
        name="fox_attn_prompt",
    )(jlo.reshape(-1), q, kTb, vTb, d, dTb, sg)


def _s_pre_conv_kernel(xT_ref, w_ref, st0_ref, st1_ref, cw_ref,
                       tokT_ref, uT_ref, qmT_ref, sgT_ref, *, tw):
    xb = xT_ref[...].T.astype(BF16)
    proj = lambda lo, hi: _dot(xb, w_ref[:, lo:hi]).T
    bg = proj(0, tw)
    u = proj(tw, 2 * tw) * proj(2 * tw, 3 * tw)
    cw = cw_ref[...]
    y = cw[:, 0:1] * st0_ref[...] + cw[:, 1:2] * st1_ref[...] + cw[:, 2:3] * u
    tokT_ref[...] = bg * y
    uT_ref[...] = u
    qmT_ref[...] = proj(3 * tw, 3 * tw + X_WIDTH) * QK_SCALE
    sgT_ref[...] = _silu(proj(3 * tw + X_WIDTH, w_ref.shape[1]))


def _s_pre_conv(xT, wT, st0T, st1T, cwT):
    d_model, nb = xT.shape
    tw = cwT.shape[0]
    f = lambda r: jax.ShapeDtypeStruct((r, nb), F32)
    return pl.pallas_call(
        functools.partial(_s_pre_conv_kernel, tw=tw),
        out_shape=[f(tw), f(tw), f(X_WIDTH), f(d_model)],
        compiler_params=pltpu.CompilerParams(vmem_limit_bytes=VMEM_LIMIT),
        name="sample_pre_conv",
    )(xT, wT, st0T, st1T, cwT)


def _s_pre_fox_kernel(xT_ref, wqkv_ref, wmg_ref, wf_ref, bf_ref,
                      qT_ref, kT_ref, vT_ref, lfT_ref, slT_ref, qmT_ref, sgT_ref, *, tw, nh):
    xb = xT_ref[...].astype(BF16)
    nb = xb.shape[1]
    hp = bf_ref.shape[0]
    q = _dot(wqkv_ref[0:tw, :], xb) * QK_SCALE
    k = _dot(wqkv_ref[tw:2 * tw, :], xb)
    qT_ref[...] = q
    kT_ref[...] = k
    vT_ref[...] = _dot(wqkv_ref[2 * tw:3 * tw, :], xb)
    qmT_ref[...] = _dot(wmg_ref[0:X_WIDTH, :], xb) * QK_SCALE
    sgT_ref[...] = _silu(_dot(wmg_ref[X_WIDTH:, :], xb))
    fT = _dot(wf_ref[...], xb) + bf_ref[...]
    hrow = lax.broadcasted_iota(jnp.int32, (hp, nb), 0)
    lfT_ref[...] = jnp.where(hrow < nh, _log_sigmoid(fT), 0.0)
    qk = q * k
    rows = [jnp.sum(qk[h * HEAD_DIM:(h + 1) * HEAD_DIM], axis=0, keepdims=True) for h in range(nh)]
    rows.append(jnp.zeros((hp - nh, nb), F32))
    slT_ref[...] = jnp.concatenate(rows, axis=0)


def _s_pre_fox(xT, wqkv, wmg, wf, bf, *, nh):
    d_model, nb = xT.shape
    tw = nh * HEAD_DIM
    hp = bf.shape[0]
    f = lambda r: jax.ShapeDtypeStruct((r, nb), F32)
    return pl.pallas_call(
        functools.partial(_s_pre_fox_kernel, tw=tw, nh=nh),
        out_shape=[f(tw), f(tw), f(tw), f(hp), f(hp), f(X_WIDTH), f(d_model)],
        compiler_params=pltpu.CompilerParams(vmem_limit_bytes=VMEM_LIMIT),
        name="sample_pre_fox",
    )(xT, wqkv, wmg, wf, bf)


def _column(ref_or_val, onehot):
    return jnp.sum(jnp.where(onehot, ref_or_val, 0.0), axis=1, keepdims=True)


def _cross_sample_cols(qcs, mks, mvs):
    pairs = [(i, h) for i in range(len(qcs)) for h in range(X_HEADS)]
    hd = lambda h: slice(h * HEAD_DIM, (h + 1) * HEAD_DIM)
    s = [jnp.sum(qcs[i][hd(h)] * mks[i][h], axis=0, keepdims=True) for i, h in pairs]
    m = [jnp.max(x, axis=1, keepdims=True) for x in s]
    e = [jnp.exp(x - y) for x, y in zip(s, m)]
    r = [1.0 / jnp.sum(x, axis=1, keepdims=True) for x in e]
    p = [x * y for x, y in zip(e, r)]
    o = [jnp.sum(x * mvs[i][h], axis=1, keepdims=True) for x, (i, h) in zip(p, pairs)]
    return [jnp.concatenate(o[i * X_HEADS:(i + 1) * X_HEADS], axis=0) for i in range(len(qcs))]


def _s_cross_kernel(qmT_ref, mk_ref, mv_ref, crsT_ref, *, bc):
    c = pl.program_id(0)

    @pl.when(c == 0)
    def _():
        crsT_ref[...] = jnp.zeros(crsT_ref.shape, F32)

    lane = lax.broadcasted_iota(jnp.int32, crsT_ref.shape, 1)
    qmT = qmT_ref[...]
    hots = [lane == c * bc + bb for bb in range(bc)]
    cols = _cross_sample_cols([_column(qmT, hot) for hot in hots],
                              [mk_ref.at[bb] for bb in range(bc)], [mv_ref.at[bb] for bb in range(bc)])
    out = crsT_ref[...]
    for bb in range(bc):
        out = jnp.where(hots[bb], cols[bb], out)
    crsT_ref[...] = out


def _s_cross(qmT, mk5, mv5, layer, *, bc):
    nb = qmT.shape[1]
    n_mem = mk5.shape[4]
    blk = pl.BlockSpec((None, bc, X_HEADS, HEAD_DIM, n_mem), lambda c: (layer, c, 0, 0, 0))
    return pl.pallas_call(
        functools.partial(_s_cross_kernel, bc=bc),
        grid=(nb // bc,),
        in_specs=[pl.BlockSpec(qmT.shape, lambda c: (0, 0)), blk, blk],
        out_specs=pl.BlockSpec(qmT.shape, lambda c: (0, 0)),
        out_shape=jax.ShapeDtypeStruct(qmT.shape, F32),
        compiler_params=pltpu.CompilerParams(dimension_semantics=("arbitrary",),
                                             vmem_limit_bytes=VMEM_LIMIT),
        name="sample_cross",
    )(qmT, mk5, mv5)


def _s_fox_scores_kernel(pt_ref, qT_ref, slT_ref, lfnT_ref, *rest, nh, n_pages, rows):
    n = rows * n_pages
    p_ref, pself_ref, l_ref, alive_ref, l_scr, s_scr = rest[2 * n:]
    for r in range(rows):
        _s_fox_scores_row(pl.program_id(0) * rows + r, qT_ref, slT_ref, lfnT_ref,
                          rest[r * n_pages:(r + 1) * n_pages], rest[n + r * n_pages:n + (r + 1) * n_pages],
                          p_ref.at[r], pself_ref.at[r], l_ref.at[r], alive_ref.at[r],
                          l_scr.at[r], s_scr.at[r], nh=nh, n_pages=n_pages)


def _s_fox_scores_row(b, qT_ref, slT_ref, lfnT_ref, lf_pages, k_pages,
                      p_ref, pself_ref, l_ref, alive_ref, l_scr, s_scr, *, nh, n_pages):
    page = k_pages[0].shape[2]
    nrow = nh * n_pages

    def onehot(shape):
        return lax.broadcasted_iota(jnp.int32, shape, 1) == b

    qc = _column(qT_ref[...], onehot(qT_ref.shape))
    slc = _column(slT_ref[...], onehot(slT_ref.shape))
    lfn = _column(lfnT_ref[...], onehot(lfnT_ref.shape))

    for p in range(n_pages):
        for h in range(nh):
            l_scr[h * n_pages + p:h * n_pages + p + 1, :] = lf_pages[p][h]
    lfp = l_scr[...]
    r = lax.broadcasted_iota(jnp.int32, (page, page), 0)
    c = lax.broadcasted_iota(jnp.int32, (page, page), 1)
    after_in_page = jnp.where(r > c, 1.0, 0.0).astype(BF16)
    w3 = _dot(jnp.concatenate(_split3(lfp), axis=0), after_in_page)
    within = w3[0:nrow] + w3[nrow:2 * nrow] + w3[2 * nrow:3 * nrow]
    tot = jnp.broadcast_to(jnp.sum(lfp, axis=1, keepdims=True), (nrow, page))
    ru = lax.broadcasted_iota(jnp.int32, (nrow, nrow), 0)
    cu = lax.broadcasted_iota(jnp.int32, (nrow, nrow), 1)
    later_page = jnp.where((cu > ru) & (cu // n_pages == ru // n_pages), 1.0, 0.0).astype(BF16)
    a3 = _dot(later_page, jnp.concatenate(_split3(tot), axis=1))
    bias = within + a3[:, 0:page] + a3[:, page:2 * page] + a3[:, 2 * page:3 * page]

    for h in range(nh):
        qh = qc[h * HEAD_DIM:(h + 1) * HEAD_DIM]
        for p in range(n_pages):
            s_scr[h * n_pages + p:h * n_pages + p + 1, :] = jnp.sum(qh * k_pages[p][h], axis=0, keepdims=True)
    s_all = s_scr[...] + bias

    hs = range(nh)
    s_h = [s_all[h * n_pages:(h + 1) * n_pages] + lfn[h:h + 1] for h in hs]
    sl = [slc[h:h + 1] for h in hs]
    m = [jnp.maximum(jnp.max(jnp.max(s_h[h], axis=1, keepdims=True), axis=0, keepdims=True), sl[h])
         for h in hs]
    p_h = [jnp.exp(s_h[h] - m[h]) for h in hs]
    p_self = [jnp.exp(sl[h] - m[h]) for h in hs]
    l = [jnp.sum(jnp.sum(p_h[h], axis=1, keepdims=True), axis=0, keepdims=True) + p_self[h] for h in hs]
    pm = [jnp.max(p_h[h], axis=1, keepdims=True) for h in hs]
    page_max = pm[0]
    for h in hs:
        p_ref[h * n_pages:(h + 1) * n_pages, :] = p_h[h]
        pself_ref[h:h + 1, :] = jnp.broadcast_to(p_self[h], (1, LANES))
        l_ref[h:h + 1, :] = jnp.broadcast_to(l[h], (1, LANES))
        page_max = jnp.maximum(page_max, pm[h])
    hp = pself_ref.shape[0]
    if hp > nh:
        pself_ref[nh:hp, :] = jnp.zeros((hp - nh, LANES), F32)
        l_ref[nh:hp, :] = jnp.ones((hp - nh, LANES), F32)
    alive_ref[...] = jnp.broadcast_to(page_max, (n_pages, LANES))


def _s_fox_values_kernel(pt_ref, p_ref, pself_ref, l_ref, vnT_ref, *rest, nh, n_pages, rows):
    v_pages = rest[0:rows * n_pages]
    tokT_ref = rest[rows * n_pages]
    c = pl.program_id(0)

    @pl.when(c == 0)
    def _():
        tokT_ref[...] = jnp.zeros(tokT_ref.shape, F32)

    lane = lax.broadcasted_iota(jnp.int32, vnT_ref.shape, 1)
    lane_h = lax.broadcasted_iota(jnp.int32, (HEAD_DIM, tokT_ref.shape[1]), 1)
    for r in range(rows):
        b = c * rows + r
        vnc = _column(vnT_ref[...], lane == b)
        sums = []
        for h in range(nh):
            acc = p_ref[r, h * n_pages:h * n_pages + 1, :] * v_pages[r * n_pages][h]
            for p in range(1, n_pages):
                acc = acc + p_ref[r, h * n_pages + p:h * n_pages + p + 1, :] * v_pages[r * n_pages + p][h]
            sums.append(jnp.sum(acc, axis=1, keepdims=True))
        for h in range(nh):
            hd = slice(h * HEAD_DIM, (h + 1) * HEAD_DIM)
            o = (sums[h] + pself_ref[r, h:h + 1, 0:1] * vnc[hd]) / l_ref[r, h:h + 1, 0:1]
            tokT_ref[hd, :] = jnp.where(lane_h == b, o, tokT_ref[hd, :])


def _s_fox(page_table, qT, vnT, slT, lfnT, lf5, k5, v5, fox_layer, *, nh):
    tw, nb = qT.shape
    n_pages = page_table.shape[1]
    page = k5.shape[4]
    hp = slT.shape[0]
    nrow = nh * n_pages
    cparams = pltpu.CompilerParams(dimension_semantics=("arbitrary",), vmem_limit_bytes=VMEM_LIMIT)

    rows = 2 if nb % 2 == 0 else 1
    slots = [(r, p) for r in range(rows) for p in range(n_pages)]
    full = lambda a: pl.BlockSpec(a.shape, lambda c, pt: (0, 0))
    per_step = lambda n, w: pl.BlockSpec((rows, n, w), lambda c, pt: (c, 0, 0))
    kv_specs = [pl.BlockSpec((None, None, nh, HEAD_DIM, page),
                             lambda c, pt, r=r, p=p: (fox_layer, pt[c * rows + r, p], 0, 0, 0)) for r, p in slots]

    lf_specs = [pl.BlockSpec((None, nh, None, 1, page),
                             lambda c, pt, r=r, p=p: (fox_layer, 0, pt[c * rows + r, p], 0, 0)) for r, p in slots]
    probs, p_self, l_sum, page_max = pl.pallas_call(
        functools.partial(_s_fox_scores_kernel, nh=nh, n_pages=n_pages, rows=rows),
        grid_spec=pltpu.PrefetchScalarGridSpec(
            num_scalar_prefetch=1,
            grid=(nb // rows,),
            in_specs=[full(qT), full(slT), full(lfnT)] + lf_specs + kv_specs,
            out_specs=[per_step(nrow, page), per_step(hp, LANES), per_step(hp, LANES),
                       per_step(n_pages, LANES)],
            scratch_shapes=[pltpu.VMEM((rows, nrow, page), F32), pltpu.VMEM((rows, nrow, page), F32)]),
        out_shape=[jax.ShapeDtypeStruct((nb, nrow, page), F32),
                   jax.ShapeDtypeStruct((nb, hp, LANES), F32),
                   jax.ShapeDtypeStruct((nb, hp, LANES), F32),
                   jax.ShapeDtypeStruct((nb, n_pages, LANES), F32)],
        compiler_params=cparams,
        name="sample_fox_scores",
    )(page_table, qT, slT, lfnT, *([lf5] * len(slots)), *([k5] * len(slots)))

    alive = (page_max[:, :, 0] > 0.0).reshape(nb // rows, rows, n_pages)
    step_id = jnp.arange(nb // rows, dtype=jnp.int32)[:, None, None]
    last_live = jnp.maximum(lax.cummax(jnp.where(alive, step_id, -1), axis=0), 0)
    pt_eff = jnp.take_along_axis(page_table.reshape(nb // rows, rows, n_pages), last_live, axis=0)

    return pl.pallas_call(
        functools.partial(_s_fox_values_kernel, nh=nh, n_pages=n_pages, rows=rows),
        grid_spec=pltpu.PrefetchScalarGridSpec(
            num_scalar_prefetch=1,
            grid=(nb // rows,),
            in_specs=[per_step(nrow, page), per_step(hp, LANES), per_step(hp, LANES), full(vnT)] + kv_specs,
            out_specs=pl.BlockSpec((tw, nb), lambda c, pt: (0, 0))),
        out_shape=jax.ShapeDtypeStruct((tw, nb), F32),
        compiler_params=cparams,
        name="sample_fox_values",
    )(pt_eff.reshape(nb, n_pages), probs, p_self, l_sum, vnT, *([v5] * len(slots)))


def _s_post_kernel(xT_ref, tokT_ref, crsT_ref, sgT_ref, wo_ref, g_ref, b_ref, out_ref, *, alpha):
    tw = tokT_ref.shape[0]
    sg = sgT_ref[...]
    mt = (tokT_ref[...] * sg[0:tw]).T.astype(BF16)
    mc = (crsT_ref[...] * sg[tw:]).T.astype(BF16)
    y = alpha * xT_ref[...] + (_dot(mt, wo_ref[0:tw, :]) + _dot(mc, wo_ref[tw:, :])).T
    out_ref[...] = _layer_norm(y, g_ref[...], b_ref[...], axis=0)


def _s_post(xT, tokT, crsT, sgT, woT, gcol, bcol, *, alpha):
    return pl.pallas_call(
        functools.partial(_s_post_kernel, alpha=alpha),
        out_shape=jax.ShapeDtypeStruct(xT.shape, F32),
        compiler_params=pltpu.CompilerParams(vmem_limit_bytes=VMEM_LIMIT),
        name="sample_post",
    )(xT, tokT, crsT, sgT, woT, gcol, bcol)


def kernel(x_prompt, x_sample, mem_prompt, cache_fox_k, cache_fox_v, cache_fox_logf, state_conv,
           cache_mem_k, cache_mem_v, page_table, w_in_conv, conv_w, w_in_fox, b_forget, w_mem_kv,
           w_out, ln_g, ln_b):
    depth = w_out.shape[0]
    alpha = float((2 * depth) ** 0.25)
    bsz, seq, d_model = x_prompt.shape
    nb = x_sample.shape[0]
    tw = conv_w.shape[2]
    nh = tw // HEAD_DIM
    hp = -(-nh // 8) * 8
    tm = min(1024, seq)
    tb = min(256, seq)
    bc = min(16, nb)
    kv_all = None

    mkT_all, mvT_all = _mem_kv(mem_prompt, jnp.swapaxes(w_mem_kv, 1, 2).astype(BF16))
    mk5 = jnp.transpose(cache_mem_k, (0, 1, 3, 4, 2))
    mv5 = jnp.transpose(cache_mem_v, (0, 1, 3, 4, 2))
    k5 = jnp.transpose(cache_fox_k, (0, 1, 3, 4, 2))
    v5 = jnp.transpose(cache_fox_v, (0, 1, 3, 4, 2))
    lf5 = jnp.transpose(cache_fox_logf, (0, 3, 1, 2))[:, :, :, None, :]

    xp = x_prompt
    xsT = x_sample.reshape(nb, d_model).T
    fl_p, fk_s, fv_s, fl_s, cv_p, cv_s = [], [], [], [], [], []

    for i in range(depth):
        j = i // 2
        woutb = w_out[i].astype(BF16)
        g_row, b_row = ln_g[i][None, :], ln_b[i][None, :]
        g_col, b_col = ln_g[i][:, None], ln_b[i][:, None]
        if i % 2 == 0:
            wb = w_in_conv[j].astype(BF16)
            xp, st_p = _conv_layer_prompt(xp, wb, conv_w[j], mkT_all[i], mvT_all[i],
                                          woutb, g_row, b_row, alpha=alpha, tm=tm)
            cv_p.append(st_p)
            st = state_conv[j]
            tokT, uT, qmT, sgT = _s_pre_conv(xsT, wb, st[:, 0].T, st[:, 1].T, conv_w[j].T)
            cv_s.append(jnp.stack([st[:, 1], uT.T], axis=1))
            crsT = _s_cross(qmT, mk5, mv5, i, bc=bc)
        else:
            wt = w_in_fox[j].T.astype(BF16)
            o_f, o_qm = 3 * tw, 3 * tw + nh
            wqkv, wmg = wt[0:o_f], wt[o_qm:]
            wf = jnp.pad(wt[o_f:o_qm], ((0, hp - nh), (0, 0)))
            bf = jnp.pad(b_forget[j], (0, hp - nh))[:, None]
            q, kT_all, vT_all, kTb, vTb, lfT, d, dTb, jlo, qm, sg = _fox_proj_prompt(
                xp, wqkv, wmg, wf, bf, kv_all, layer=j, n_layers=depth // 2, tm=tm, tb=tb, nh=nh)
            kv_all = (kT_all, vT_all)
            mt = _fox_attn_prompt(jlo, q, kTb, vTb, d, dTb, sg)
            xp = _fox_finish_prompt(xp, mt, qm, sg, mkT_all[i], mvT_all[i], woutb, g_row, b_row,
                                    alpha=alpha, tm=tm)
            fl_p.append(jnp.transpose(lfT[:, :nh, :], (0, 2, 1)))

            qT, knT, vnT, lfnT, slT, qmT, sgT = _s_pre_fox(xsT, wqkv, wmg, wf, bf, nh=nh)
            tokT = _s_fox(page_table, qT, vnT, slT, lfnT, lf5, k5, v5, j, nh=nh)
            crsT = _s_cross(qmT, mk5, mv5, i, bc=bc)
            fk_s.append(knT.T.reshape(nb, 1, nh, HEAD_DIM))
            fv_s.append(vnT.T.reshape(nb, 1, nh, HEAD_DIM))
            fl_s.append(lfnT[:nh].T.reshape(nb, 1, nh))
        xsT = _s_post(xsT, tokT, crsT, sgT, woutb, g_col, b_col, alpha=alpha)

    n_mem = mem_prompt.shape[1]
    mk_out = jnp.transpose(mkT_all.reshape(depth, bsz, X_HEADS, HEAD_DIM, n_mem), (0, 1, 4, 2, 3))
    mv_out = jnp.transpose(mvT_all.reshape(depth, bsz, X_HEADS, HEAD_DIM, n_mem), (0, 1, 4, 2, 3))
    n_fox = depth // 2
    fk_all = jnp.transpose(kv_all[0].reshape(n_fox, bsz, nh, HEAD_DIM, seq), (0, 1, 4, 2, 3))
    fv_all = jnp.transpose(kv_all[1].reshape(n_fox, bsz, nh, HEAD_DIM, seq), (0, 1, 4, 2, 3))
    return (xp, xsT.T.reshape(nb, 1, d_model),
            fk_all, fv_all, jnp.stack(fl_p),
            jnp.stack(fk_s), jnp.stack(fv_s), jnp.stack(fl_s),
            jnp.stack(cv_p), jnp.stack(cv_s), mk_out, mv_out)
```
